```python
import math
import jax, jax.numpy as jnp
from jax import lax
import numpy as np

D_MODEL = 2048
BATCH = 1
SEQ = 16384
DEPTH = 2

FOX_HEADS = 8
FOX_HEAD_DIM = 128
FOX_WIDTH = FOX_HEADS * FOX_HEAD_DIM
Q_BLOCK = 128
SSM_WIDTH = D_MODEL // 2
SSM_GROUP = 16
SSM_GROUPS = SSM_WIDTH // SSM_GROUP
SSM_STATE = 64
SWA_HEAD_DIM = 64
SWA_Q_HEADS = D_MODEL // SWA_HEAD_DIM
SWA_Q_PER_KV = 8
SWA_KV_HEADS = SWA_Q_HEADS // SWA_Q_PER_KV
WINDOW = 128
SWA_BLOCK = WINDOW
ROT_DIM = SWA_HEAD_DIM // 4
ROPE_THETA = 500000.0
D_FF = ((8 * D_MODEL + 3 * 256 - 1) // (3 * 256)) * 256
DEEPNORM_ALPHA = (2 * DEPTH) ** 0.25
DEEPNORM_BETA = (8 * DEPTH) ** -0.25
LN_EPS = 1e-5
N_EVEN = (DEPTH + 1) // 2
N_ODD = DEPTH // 2
EVEN_IN = 3 * FOX_WIDTH + FOX_HEADS + SSM_WIDTH
ODD_IN = (SWA_Q_HEADS + 2 * SWA_KV_HEADS) * SWA_HEAD_DIM

kernel_name = 'hybrid_fox_s5_swa_deepnorm_adaln'


def layer_norm(x, g, b):
    xf = x.astype(jnp.float32)
    mu = jnp.mean(xf, axis=-1, keepdims=True)
    var = jnp.mean(jnp.square(xf - mu), axis=-1, keepdims=True)
    y = (xf - mu) * lax.rsqrt(var + LN_EPS) * g.astype(jnp.float32) + b.astype(jnp.float32)
    return y.astype(x.dtype)


def partial_rotary(x, positions):
    half = ROT_DIM // 2
    inv_freq = jnp.power(jnp.float32(ROPE_THETA), -jnp.arange(half, dtype=jnp.float32) * (2.0 / ROT_DIM))
    ang = positions.astype(jnp.float32)[:, :, None] * inv_freq
    cos = jnp.cos(ang)[:, :, None, :]
    sin = jnp.sin(ang)[:, :, None, :]
    xf = x.astype(jnp.float32)
    x1 = xf[..., :half]
    x2 = xf[..., half:ROT_DIM]
    out = jnp.concatenate([x1 * cos - x2 * sin, x2 * cos + x1 * sin, xf[..., ROT_DIM:]], axis=-1)
    return out.astype(x.dtype)


def forgetting_attention(q, k, v, log_f):
    B, L, H, d = q.shape
    nb = L // Q_BLOCK
    scale = 1.0 / math.sqrt(d)
    Fk = lax.cumsum(log_f, axis=1).transpose(0, 2, 1)
    Fq_blocks = Fk.reshape(B, H, nb, Q_BLOCK).transpose(2, 0, 1, 3)
    q_blocks = q.reshape(B, nb, Q_BLOCK, H, d).transpose(1, 0, 2, 3, 4)
    kpos = jnp.arange(L)

    def one_block(args):
        qb, Fq, n = args
        s = jnp.einsum('bqhd,bshd->bhqs', qb, k).astype(jnp.float32) * scale
        s = s + Fq[:, :, :, None] - Fk[:, :, None, :]
        qpos = n * Q_BLOCK + jnp.arange(Q_BLOCK)
        causal = kpos[None, :] <= qpos[:, None]
        s = jnp.where(causal[None, None], s, -jnp.inf)
        p = jax.nn.softmax(s, axis=-1)
        return jnp.einsum('bhqs,bshd->bqhd', p.astype(v.dtype), v)

    out = lax.map(one_block, (q_blocks, Fq_blocks, jnp.arange(nb)))
    return out.transpose(1, 0, 2, 3, 4).reshape(B, L, H * d)


def _complex_affine_combine(e1, e2):
    a1r, a1i, b1r, b1i = e1
    a2r, a2i, b2r, b2i = e2
    ar = a2r * a1r - a2i * a1i
    ai = a2r * a1i + a2i * a1r
    br = a2r * b1r - a2i * b1i + b2r
    bi = a2r * b1i + a2i * b1r + b2i
    return (ar, ai, br, bi)


def s5_ssm(u, lam_re, lam_im, log_dt, b_re, b_im, c_re, c_im, d_skip):
    B_, L, _ = u.shape
    f32 = jnp.float32
    uf = u.astype(f32).reshape(B_, L, SSM_GROUPS, SSM_GROUP)
    lam_re = lam_re.astype(f32)
    lam_im = lam_im.astype(f32)
    dt = jnp.exp(log_dt.astype(f32))[:, None]
    mag = jnp.exp(lam_re * dt)
    lb_re = mag * jnp.cos(lam_im * dt)
    lb_im = mag * jnp.sin(lam_im * dt)
    den = lam_re * lam_re + lam_im * lam_im
    nr = lb_re - 1.0
    q_re = (nr * lam_re + lb_im * lam_im) / den
    q_im = (lb_im * lam_re - nr * lam_im) / den
    br = b_re.astype(f32)
    bi = b_im.astype(f32)
    bb_re = q_re[..., None] * br - q_im[..., None] * bi
    bb_im = q_re[..., None] * bi + q_im[..., None] * br
    bu_re = jnp.einsum('blgh,gph->blgp', uf, bb_re)
    bu_im = jnp.einsum('blgh,gph->blgp', uf, bb_im)
    a_re = jnp.broadcast_to(lb_re, bu_re.shape)
    a_im = jnp.broadcast_to(lb_im, bu_im.shape)
    _, _, x_re, x_im = lax.associative_scan(_complex_affine_combine, (a_re, a_im, bu_re, bu_im), axis=1)
    y = (jnp.einsum('ghp,blgp->blgh', c_re.astype(f32), x_re)
         - jnp.einsum('ghp,blgp->blgh', c_im.astype(f32), x_im)
         + d_skip.astype(f32) * uf)
    return y.reshape(B_, L, SSM_WIDTH)


def even_mixer(h, w_in, b_forget, lam_re, lam_im, log_dt, b_re, b_im, c_re, c_im, d_skip, w_glu, b_glu, w_out):
    B, L, _ = h.shape
    proj = h @ w_in
    W = FOX_WIDTH
    q = proj[..., 0:W].reshape(B, L, FOX_HEADS, FOX_HEAD_DIM)
    k = proj[..., W:2 * W].reshape(B, L, FOX_HEADS, FOX_HEAD_DIM)
    v = proj[..., 2 * W:3 * W].reshape(B, L, FOX_HEADS, FOX_HEAD_DIM)
    f_logit = proj[..., 3 * W:3 * W + FOX_HEADS].astype(jnp.float32) + b_forget.astype(jnp.float32)
    u = proj[..., 3 * W + FOX_HEADS:]
    a_out = forgetting_attention(q, k, v, jax.nn.log_sigmoid(f_logit))
    s = jax.nn.gelu(s5_ssm(u, lam_re, lam_im, log_dt, b_re, b_im, c_re, c_im, d_skip))
    s = s * jax.nn.sigmoid(s @ w_glu.astype(jnp.float32) + b_glu.astype(jnp.float32))
    return jnp.concatenate([a_out, s.astype(h.dtype)], axis=-1) @ w_out


def sliding_window_sink_attention(q, k, v, sinks):
    B, L, Hq, dh = q.shape
    nb = L // SWA_BLOCK
    scale = 1.0 / math.sqrt(dh)
    qb = q.reshape(B, nb, SWA_BLOCK, SWA_KV_HEADS, SWA_Q_PER_KV, dh)
    kb = k.reshape(B, nb, SWA_BLOCK, SWA_KV_HEADS, dh)
    vb = v.reshape(B, nb, SWA_BLOCK, SWA_KV_HEADS, dh)
    zk = jnp.zeros_like(kb[:, :1])
    kk = jnp.concatenate([jnp.concatenate([zk, kb[:, :-1]], axis=1), kb], axis=2)
    vv = jnp.concatenate([jnp.concatenate([zk, vb[:, :-1]], axis=1), vb], axis=2)
    s = jnp.einsum('bnqkgd,bnskd->bnkgqs', qb, kk).astype(jnp.float32) * scale
    qi = jnp.arange(SWA_BLOCK)[:, None]
    kj = jnp.arange(2 * SWA_BLOCK)[None, :]
    rel = qi + SWA_BLOCK - kj
    band = (rel >= 0) & (rel < WINDOW)
    valid = (jnp.arange(nb)[:, None, None] * SWA_BLOCK + kj[None] - SWA_BLOCK) >= 0
    mask = band[None] & valid
    s = jnp.where(mask[None, :, None, None], s, -jnp.inf)
    sink = sinks.astype(jnp.float32).reshape(SWA_KV_HEADS, SWA_Q_PER_KV)[None, None, :, :, None, None]
    m = jnp.maximum(jnp.max(s, axis=-1, keepdims=True), sink)
    p = jnp.exp(s - m)
    p = p / (jnp.sum(p, axis=-1, keepdims=True) + jnp.exp(sink - m))
    out = jnp.einsum('bnkgqs,bnskd->bnqkgd', p.astype(v.dtype), vv)
    return out.reshape(B, L, Hq * dh)


def odd_mixer(h, positions, w_in, sinks, w_out):
    B, L, _ = h.shape
    proj = h @ w_in
    qw = SWA_Q_HEADS * SWA_HEAD_DIM
    kw = SWA_KV_HEADS * SWA_HEAD_DIM
    q = proj[..., :qw].reshape(B, L, SWA_Q_HEADS, SWA_HEAD_DIM)
    k = proj[..., qw:qw + kw].reshape(B, L, SWA_KV_HEADS, SWA_HEAD_DIM)
    v = proj[..., qw + kw:].reshape(B, L, SWA_KV_HEADS, SWA_HEAD_DIM)
    q = partial_rotary(q, positions)
    k = partial_rotary(k, positions)
    return sliding_window_sink_attention(q, k, v, sinks) @ w_out


def swiglu(h, w_gate, w_up, w_down):
    return (jax.nn.silu(h @ w_gate) * (h @ w_up)) @ w_down


def setup_inputs(seed: int = 0) -> dict:
    key = jax.random.key(seed)
    ks = jax.random.split(key, 32)
    f32 = jnp.float32

    def nrm(k, shape, scale):
        return jax.random.normal(k, shape, f32) * scale

    G, P, H = SSM_GROUPS, SSM_STATE, SSM_GROUP
    n_idx = jnp.arange(P, dtype=f32)
    mix_w = FOX_WIDTH + SSM_WIDTH
    return {
        'x': nrm(ks[0], (BATCH, SEQ, D_MODEL), 1.0),
        'c': nrm(ks[1], (BATCH, D_MODEL), 1.0),
        'positions': jnp.broadcast_to(jnp.arange(SEQ, dtype=jnp.int32)[None, :], (BATCH, SEQ)),
        'w_in_ab': nrm(ks[2], (N_EVEN, D_MODEL, EVEN_IN), D_MODEL ** -0.5),
        'b_forget': 3.0 + nrm(ks[3], (N_EVEN, FOX_HEADS), 0.1),
        'ssm_lambda_re': -0.5 + nrm(ks[4], (N_EVEN, G, P), 0.01),
        'ssm_lambda_im': math.pi * n_idx + nrm(ks[5], (N_EVEN, G, P), 0.01),
        'ssm_log_dt': jax.random.uniform(ks[6], (N_EVEN, G), f32, math.log(1e-3), math.log(1e-1)),
        'ssm_b_re': nrm(ks[7], (N_EVEN, G, P, H), (2 * H) ** -0.5),
        'ssm_b_im': nrm(ks[8], (N_EVEN, G, P, H), (2 * H) ** -0.5),
        'ssm_c_re': nrm(ks[9], (N_EVEN, G, H, P), P ** -0.5),
        'ssm_c_im': nrm(ks[10], (N_EVEN, G, H, P), P ** -0.5),
        'ssm_d': nrm(ks[11], (N_EVEN, G, H), 1.0),
        'w_glu': nrm(ks[12], (N_EVEN, SSM_WIDTH, SSM_WIDTH), SSM_WIDTH ** -0.5),
        'b_glu': nrm(ks[13], (N_EVEN, SSM_WIDTH), 0.01),
        'w_out_ab': nrm(ks[14], (N_EVEN, mix_w, D_MODEL), DEEPNORM_BETA * mix_w ** -0.5),
        'w_in_c': nrm(ks[15], (N_ODD, D_MODEL, ODD_IN), D_MODEL ** -0.5),
        'attn_sinks': nrm(ks[16], (N_ODD, SWA_Q_HEADS), 0.5),
        'w_out_c': nrm(ks[17], (N_ODD, SWA_Q_HEADS * SWA_HEAD_DIM, D_MODEL), DEEPNORM_BETA * (SWA_Q_HEADS * SWA_HEAD_DIM) ** -0.5),
        'w_ada': nrm(ks[18], (DEPTH, D_MODEL, 6 * D_MODEL), 0.5 * D_MODEL ** -0.5),
        'b_ada': nrm(ks[19], (DEPTH, 6 * D_MODEL), 0.01),
        'ln_mix_g': 1.0 + nrm(ks[20], (DEPTH, D_MODEL), 0.02),
        'ln_mix_b': nrm(ks[21], (DEPTH, D_MODEL), 0.01),
        'ln_ffn_g': 1.0 + nrm(ks[22], (DEPTH, D_MODEL), 0.02),
        'ln_ffn_b': nrm(ks[23], (DEPTH, D_MODEL), 0.01),
        'w_ffn_gate': nrm(ks[24], (DEPTH, D_MODEL, D_FF), D_MODEL ** -0.5),
        'w_ffn_up': nrm(ks[25], (DEPTH, D_MODEL, D_FF), D_MODEL ** -0.5),
        'w_ffn_down': nrm(ks[26], (DEPTH, D_FF, D_MODEL), DEEPNORM_BETA * D_FF ** -0.5),
    }


def reference(x, c, positions, w_in_ab, b_forget, ssm_lambda_re, ssm_lambda_im, ssm_log_dt,
              ssm_b_re, ssm_b_im, ssm_c_re, ssm_c_im, ssm_d, w_glu, b_glu, w_out_ab,
              w_in_c, attn_sinks, w_out_c, w_ada, b_ada, ln_mix_g, ln_mix_b, ln_ffn_g, ln_ffn_b,
              w_ffn_gate, w_ffn_up, w_ffn_down):
    for layer in range(DEPTH):
        mod = jax.nn.silu(c) @ w_ada[layer] + b_ada[layer]
        sh1, sc1, g1, sh2, sc2, g2 = jnp.split(mod, 6, axis=-1)
        h = x * (1.0 + sc1[:, None, :]) + sh1[:, None, :]
        i = layer // 2
        if layer % 2 == 0:
            y = even_mixer(h, w_in_ab[i], b_forget[i], ssm_lambda_re[i], ssm_lambda_im[i], ssm_log_dt[i],
                           ssm_b_re[i], ssm_b_im[i], ssm_c_re[i], ssm_c_im[i], ssm_d[i],
                           w_glu[i], b_glu[i], w_out_ab[i])
        else:
            y = odd_mixer(h, positions, w_in_c[i], attn_sinks[i], w_out_c[i])
        x = layer_norm(DEEPNORM_ALPHA * x + g1[:, None, :] * y, ln_mix_g[layer], ln_mix_b[layer])
        h = x * (1.0 + sc2[:, None, :]) + sh2[:, None, :]
        y = swiglu(h, w_ffn_gate[layer], w_ffn_up[layer], w_ffn_down[layer])
        x = layer_norm(DEEPNORM_ALPHA * x + g2[:, None, :] * y, ln_ffn_g[layer], ln_ffn_b[layer])
    return x
```

```python
import functools
import math

import jax
import jax.numpy as jnp
from jax import lax
from jax.experimental import pallas as pl
from jax.experimental.pallas import tpu as pltpu

F32 = jnp.float32
BF16 = jnp.bfloat16

FOX_HEADS = 8
FOX_HEAD_DIM = 128
FOX_WIDTH = FOX_HEADS * FOX_HEAD_DIM
SSM_GROUP = 16
SSM_STATE = 64
SWA_HEAD_DIM = 64
SWA_Q_PER_KV = 8
WINDOW = 128
ROT_DIM = SWA_HEAD_DIM // 4
ROPE_THETA = 500000.0
DEPTH = 2
DEEPNORM_ALPHA = (2 * DEPTH) ** 0.25
LN_EPS = 1e-5

LANES = 128
SUBLANES = 8
NEG_BIG = -1e30

SSM_CHUNK = 16
SSM_BLOCK_GROUPS = LANES // SSM_GROUP


def _cparams(n_axes, vmem_mb):
    return pltpu.CompilerParams(
        dimension_semantics=("arbitrary",) * n_axes,
        vmem_limit_bytes=vmem_mb * 1024 * 1024)


def _sigmoid(x):
    return 1.0 / (1.0 + jnp.exp(-x))


def _layer_norm(r, g, b):
    mu = jnp.mean(r, axis=-1, keepdims=True)
    d = r - mu
    var = jnp.mean(d * d, axis=-1, keepdims=True)
    return d * lax.rsqrt(var + LN_EPS) * g + b


def _dot(a, b):
    return jnp.dot(a, b, preferred_element_type=F32)


def _dot_nt(a, b):
    return lax.dot_general(a, b, (((1,), (1,)), ((), ())), preferred_element_type=F32)


def _ada_kernel(c_ref, w_ref, b_ref, o_ref):
    c = c_ref[...]
    sc = c * _sigmoid(c)
    o_ref[0] = jnp.dot(sc, w_ref[0], precision=lax.Precision.HIGHEST,
                       preferred_element_type=F32) + b_ref[0]


def _ada(c, w_ada, b_ada):
    depth, d, n = w_ada.shape
    tn = 1536
    c8 = jnp.broadcast_to(c, (SUBLANES, d))
    out = pl.pallas_call(
        _ada_kernel,
        grid=(depth, n // tn),
        in_specs=[pl.BlockSpec((SUBLANES, d), lambda l, j: (0, 0)),
                  pl.BlockSpec((1, d, tn), lambda l, j: (l, 0, j)),
                  pl.BlockSpec((1, 1, tn), lambda l, j: (l, 0, j))],
        out_specs=pl.BlockSpec((1, SUBLANES, tn), lambda l, j: (l, 0, j)),
        out_shape=jax.ShapeDtypeStruct((depth, SUBLANES, n), F32),
        compiler_params=_cparams(2, 40),
        name="ada",
    )(c8, w_ada, b_ada.reshape(depth, 1, n))
    return out[:, 0:1, :]


def _inproj0_kernel(x_ref, sc_ref, sh_ref, w_ref, wf_ref, bf_ref, o_ref, f_ref,
                    h_sc, carry_sc, *, tm):
    i = pl.program_id(0)
    j = pl.program_id(1)

    @pl.when(j == 0)
    def _():
        hb = (x_ref[...] * (1.0 + sc_ref[...]) + sh_ref[...]).astype(BF16)
        h_sc[...] = hb
        z = _dot(hb, wf_ref[...]) + bf_ref[...]
        lf = jnp.minimum(z, 0.0) - jnp.log1p(jnp.exp(-jnp.abs(z)))
        row = lax.broadcasted_iota(jnp.int32, (tm, tm), 0)
        col = lax.broadcasted_iota(jnp.int32, (tm, tm), 1)
        tri = (col <= row).astype(BF16)
        p1 = lf.astype(BF16)
        r1 = lf - p1.astype(F32)
        p2 = r1.astype(BF16)
        p3 = (r1 - p2.astype(F32)).astype(BF16)
        cum = _dot(tri, p1) + _dot(tri, p2) + _dot(tri, p3)

        @pl.when(i == 0)
        def _():
            carry_sc[...] = jnp.zeros_like(carry_sc)

        f = cum + carry_sc[...]
        f_ref[...] = f
        carry_sc[...] = f[tm - 1:tm, :]

    o_ref[...] = _dot(h_sc[...], w_ref[...]).astype(BF16)


def _inproj0(x, sc, sh, w, wf, bf):
    L, d = x.shape
    n = w.shape[1]
    tm, tn = 512, 1024
    kern = functools.partial(_inproj0_kernel, tm=tm)
    return pl.pallas_call(
        kern,
        grid=(L // tm, n // tn),
        in_specs=[pl.BlockSpec((tm, d), lambda i, j: (i, 0)),
                  pl.BlockSpec((1, d), lambda i, j: (0, 0)),
                  pl.BlockSpec((1, d), lambda i, j: (0, 0)),
                  pl.BlockSpec((d, tn), lambda i, j: (0, j)),
                  pl.BlockSpec((d, LANES), lambda i, j: (0, 0)),
                  pl.BlockSpec((1, LANES), lambda i, j: (0, 0))],
        out_specs=[pl.BlockSpec((tm, tn), lambda i, j: (i, j)),
                   pl.BlockSpec((tm, LANES), lambda i, j: (i, 0))],
        out_shape=[jax.ShapeDtypeStruct((L, n), BF16),
                   jax.ShapeDtypeStruct((L, LANES), F32)],
        scratch_shapes=[pltpu.VMEM((tm, d), BF16), pltpu.VMEM((1, LANES), F32)],
        compiler_params=_cparams(2, 40),
        name="inproj0",
    )(x, sc, sh, w, wf, bf)


def _fox_kernel(q_ref, k_ref, v_ref, fq_ref, fk_ref, o_ref, m_sc, l_sc, acc_sc,
                *, tq, scale):
    i = pl.program_id(1)
    q = (q_ref[...].astype(F32) * scale).astype(BF16)
    fq = fq_ref[0]
    m_sc[...] = jnp.full_like(m_sc, NEG_BIG)
    l_sc[...] = jnp.zeros_like(l_sc)
    acc_sc[...] = jnp.zeros_like(acc_sc)

    def block(j, masked):
        start = pl.multiple_of(j * tq, tq)
        ks = k_ref[pl.ds(start, tq), :]
        vs = v_ref[pl.ds(start, tq), :]
        s = _dot_nt(q, ks) + fq - fk_ref[0, j]
        if masked:
            row = lax.broadcasted_iota(jnp.int32, (tq, tq), 0)
            col = lax.broadcasted_iota(jnp.int32, (tq, tq), 1)
            s = jnp.where(col <= row, s, NEG_BIG)
        m_prev = m_sc[...]
        m_new = jnp.maximum(m_prev, jnp.max(s, axis=1, keepdims=True))
        alpha = jnp.exp(m_prev - m_new)
        p = jnp.exp(s - m_new)
        l_sc[...] = alpha * l_sc[...] + jnp.sum(p, axis=1, keepdims=True)
        acc_sc[...] = alpha * acc_sc[...] + _dot(p.astype(BF16), vs)
        m_sc[...] = m_new

    def body(j, carry):
        block(j, False)
        return carry

    lax.fori_loop(0, i, body, 0)
    block(i, True)
    o_ref[...] = (acc_sc[...] / l_sc[...]).astype(BF16)


def _fox(qkvu, fcol, frow):
    L = qkvu.shape[0]
    tq = 512
    nk = L // tq
    hb = FOX_WIDTH // FOX_HEAD_DIM
    kern = functools.partial(_fox_kernel, tq=tq, scale=1.0 / math.sqrt(FOX_HEAD_DIM))
    return pl.pallas_call(
        kern,
        grid=(FOX_HEADS, L // tq),
        in_specs=[pl.BlockSpec((tq, FOX_HEAD_DIM), lambda h, i: (i, h)),
                  pl.BlockSpec((L, FOX_HEAD_DIM), lambda h, i: (0, hb + h)),
                  pl.BlockSpec((L, FOX_HEAD_DIM), lambda h, i: (0, 2 * hb + h)),
                  pl.BlockSpec((1, tq, 1), lambda h, i: (h, i, 0)),
                  pl.BlockSpec((1, nk, 1, tq), lambda h, i: (h, 0, 0, 0))],
        out_specs=pl.BlockSpec((tq, FOX_HEAD_DIM), lambda h, i: (i, h)),
        out_shape=jax.ShapeDtypeStruct((L, FOX_WIDTH), BF16),
        scratch_shapes=[pltpu.VMEM((tq, 1), F32), pltpu.VMEM((tq, 1), F32),
                        pltpu.VMEM((tq, FOX_HEAD_DIM), F32)],
        compiler_params=_cparams(2, 48),
        name="fox",
    )(qkvu, qkvu, qkvu, fcol, frow)


def _taps_kernel(cre_ref, cim_ref, pre_ref, pim_ref, bre_ref, bim_ref, o_ref):
    T, H, P = SSM_CHUNK, SSM_GROUP, SSM_STATE
    for g in range(SSM_BLOCK_GROUPS):
        cre = cre_ref[g][None]
        cim = cim_ref[g][None]
        pre = pre_ref[g][:, None, :]
        pim = pim_ref[g][:, None, :]
        wre = (cre * pre - cim * pim).reshape(T * H, P)
        wim = (cre * pim + cim * pre).reshape(T * H, P)
        k = (jnp.dot(wre, bre_ref[g], precision=lax.Precision.HIGHEST,
                     preferred_element_type=F32)
             - jnp.dot(wim, bim_ref[g], precision=lax.Precision.HIGHEST,
                       preferred_element_type=F32))
        o_ref[g] = k


def _taps(c_re, c_im, pw_re, pw_im, bb_re, bb_im):
    G = c_re.shape[0]
    gb = SSM_BLOCK_GROUPS
    T, H, P = SSM_CHUNK, SSM_GROUP, SSM_STATE
    return pl.pallas_call(
        _taps_kernel,
        grid=(G // gb,),
        in_specs=[pl.BlockSpec((gb, H, P), lambda b: (b, 0, 0)),
                  pl.BlockSpec((gb, H, P), lambda b: (b, 0, 0)),
                  pl.BlockSpec((gb, T, P), lambda b: (b, 0, 0)),
                  pl.BlockSpec((gb, T, P), lambda b: (b, 0, 0)),
                  pl.BlockSpec((gb, P, H), lambda b: (b, 0, 0)),
                  pl.BlockSpec((gb, P, H), lambda b: (b, 0, 0))],
        out_specs=pl.BlockSpec((gb, T * H, H), lambda b: (b, 0, 0)),
        out_shape=jax.ShapeDtypeStruct((G, T * H, H), F32),
        compiler_params=_cparams(1, 32),
        name="taps",
    )(c_re, c_im, pw_re, pw_im, bb_re, bb_im)


def _ssm_kernel(u_ref, m1_ref, m2_ref, m3_ref, ar_ref, ai_ref, pr_ref, pi_ref,
                a2r_ref, a2i_ref, o_ref, ucat_sc, s_sc, x_sc, *, nrow):
    tp = pl.program_id(1)
    T = SSM_CHUNK
    half = x_sc.shape[1] // 2
    ngrp = nrow // SUBLANES

    @pl.when(tp == 0)
    def _():
        for s in range(T):
            ucat_sc[:, s * LANES:(s + 1) * LANES] = u_ref[s]
        s_sc[...] = _dot(ucat_sc[...], m3_ref[0])
        ar = ar_ref[0]
        ai = ai_ref[0]

        def step(r, carry):
            xr, xi = carry
            row = pl.multiple_of(r * SUBLANES, SUBLANES)
            x_sc[pl.ds(row, SUBLANES), 0:half] = xr
            x_sc[pl.ds(row, SUBLANES), half:2 * half] = xi
            sr = s_sc[pl.ds(row, SUBLANES), 0:half]
            si = s_sc[pl.ds(row, SUBLANES), half:2 * half]
            return ar * xr - ai * xi + sr, ar * xi + ai * xr + si

        zero = jnp.zeros((SUBLANES, half), F32)
        er, ei = lax.fori_loop(0, ngrp, step, (zero, zero))

        a2r = a2r_ref[0]
        a2i = a2i_ref[0]
        sub = lax.broadcasted_iota(jnp.int32, (SUBLANES, half), 0)
        zr = jnp.zeros((1, half), F32)
        zi = jnp.zeros((1, half), F32)
        zr_all = zero
        zi_all = zero
        for jj in range(1, SUBLANES):
            nzr = a2r * zr - a2i * zi + er[jj - 1:jj]
            nzi = a2r * zi + a2i * zr + ei[jj - 1:jj]
            zr, zi = nzr, nzi
            zr_all = jnp.where(sub == jj, zr, zr_all)
            zi_all = jnp.where(sub == jj, zi, zi_all)

        def fix(r, carry):
            row = pl.multiple_of(r * SUBLANES, SUBLANES)
            pr = pr_ref[0, pl.ds(r, 1), :]
            pi = pi_ref[0, pl.ds(r, 1), :]
            x_sc[pl.ds(row, SUBLANES), 0:half] = (
                x_sc[pl.ds(row, SUBLANES), 0:half] + pr * zr_all - pi * zi_all)
            x_sc[pl.ds(row, SUBLANES), half:2 * half] = (
                x_sc[pl.ds(row, SUBLANES), half:2 * half] + pr * zi_all + pi * zr_all)
            return carry

        lax.fori_loop(0, ngrp, fix, 0)

    y = _dot(ucat_sc[...], m1_ref[0]) + _dot(x_sc[...].astype(BF16), m2_ref[0])
    o_ref[0] = y[:, 0:LANES].astype(BF16)
    o_ref[1] = y[:, LANES:2 * LANES].astype(BF16)


def _ssm(u_s, m1, m2, m3, a_re, a_im, p_re, p_im, a2_re, a2_im):
    T, nrow, width = u_s.shape
    nb = width // LANES
    nstate = m3.shape[2]
    half = nstate // 2
    ngrp = nrow // SUBLANES
    kern = functools.partial(_ssm_kernel, nrow=nrow)
    return pl.pallas_call(
        kern,
        grid=(nb, T // 2),
        in_specs=[pl.BlockSpec((T, nrow, LANES), lambda b, t: (0, 0, b)),
                  pl.BlockSpec((1, T * LANES, 2 * LANES), lambda b, t: (b, 0, t)),
                  pl.BlockSpec((1, nstate, 2 * LANES), lambda b, t: (b, 0, t)),
                  pl.BlockSpec((1, T * LANES, nstate), lambda b, t: (b, 0, 0)),
                  pl.BlockSpec((1, 1, half), lambda b, t: (b, 0, 0)),
                  pl.BlockSpec((1, 1, half), lambda b, t: (b, 0, 0)),
                  pl.BlockSpec((1, ngrp, half), lambda b, t: (b, 0, 0)),
                  pl.BlockSpec((1, ngrp, half), lambda b, t: (b, 0, 0)),
                  pl.BlockSpec((1, 1, half), lambda b, t: (b, 0, 0)),
                  pl.BlockSpec((1, 1, half), lambda b, t: (b, 0, 0))],
        out_specs=pl.BlockSpec((2, nrow, LANES), lambda b, t: (t, 0, b)),
        out_shape=jax.ShapeDtypeStruct((T, nrow, width), BF16),
        scratch_shapes=[pltpu.VMEM((nrow, T * LANES), BF16),
                        pltpu.VMEM((nrow, nstate), F32),
                        pltpu.VMEM((nrow, nstate), F32)],
        compiler_params=_cparams(2, 48),
        name="ssm",
    )(u_s, m1, m2, m3, a_re, a_im, p_re, p_im, a2_re, a2_im)


def _ssm_tables(lam_re, lam_im, log_dt, b_re, b_im, c_re, c_im, d_skip, ngrp):
    G, P = lam_re.shape
    H = SSM_GROUP
    T = SSM_CHUNK
    gb = SSM_BLOCK_GROUPS
    nb = G // gb
    dt = jnp.exp(log_dt)[:, None]
    mag = jnp.exp(lam_re * dt)
    lb_re = mag * jnp.cos(lam_im * dt)
    lb_im = mag * jnp.sin(lam_im * dt)
    den = lam_re * lam_re + lam_im * lam_im
    nr = lb_re - 1.0
    q_re = (nr * lam_re + lb_im * lam_im) / den
    q_im = (lb_im * lam_re - nr * lam_im) / den
    bb_re = q_re[..., None] * b_re - q_im[..., None] * b_im
    bb_im = q_re[..., None] * b_im + q_im[..., None] * b_re

    def cmul(ar, ai, br, bi):
        return ar * br - ai * bi, ar * bi + ai * br

    def powers(ar, ai, n):
        pr, pi = jnp.ones_like(ar)[None], jnp.zeros_like(ai)[None]
        sr, si = ar, ai
        while pr.shape[0] < n:
            nr_, ni_ = cmul(pr, pi, sr[None], si[None])
            pr = jnp.concatenate([pr, nr_], 0)
            pi = jnp.concatenate([pi, ni_], 0)
            sr, si = cmul(sr, si, sr, si)
        return pr[:n], pi[:n], sr, si

    pw_re, pw_im, _, _ = powers(lb_re, lb_im, 2 * T)
    aT_re, aT_im = pw_re[T], pw_im[T]
    pT_re, pT_im, a2_re, a2_im = powers(aT_re, aT_im, ngrp)

    taps = _taps(c_re, c_im,
                 jnp.transpose(pw_re[:T], (1, 0, 2)), jnp.transpose(pw_im[:T], (1, 0, 2)),
                 bb_re, bb_im)
    kd = taps.reshape(G, T, H, H)
    kd = kd.at[:, 0].add(d_skip[:, :, None] * jnp.eye(H, dtype=F32))
    kpad = jnp.concatenate([kd, jnp.zeros((G, 1, H, H), F32)], 1)
    tt = jnp.arange(T)
    idx = jnp.where(tt[None, :] >= tt[:, None], tt[None, :] - tt[:, None], T)
    ktoe = kpad[:, idx]
    eye = jnp.eye(gb, dtype=F32)
    m1 = jnp.einsum("bgstij,gh->bsgjthi", ktoe.reshape(nb, gb, T, T, H, H), eye)
    m1 = m1.reshape(nb, T * LANES, T * LANES).astype(BF16)

    er = pw_re[:T][::-1]
    ei = pw_im[:T][::-1]
    v_re = er[..., None] * bb_re[None] - ei[..., None] * bb_im[None]
    v_im = er[..., None] * bb_im[None] + ei[..., None] * bb_re[None]

    def place3(v):
        v = v.reshape(T, nb, gb, P, H)
        return jnp.einsum("sbgpj,gh->bsgjhp", v, eye).reshape(nb, T * LANES, gb * P)

    m3 = jnp.concatenate([place3(v_re), place3(v_im)], -1).astype(BF16)

    fr = pw_re[1:T + 1]
    fi = pw_im[1:T + 1]
    w_re = c_re[None] * fr[:, :, None, :] - c_im[None] * fi[:, :, None, :]
    w_im = c_re[None] * fi[:, :, None, :] + c_im[None] * fr[:, :, None, :]

    def place2(w):
        w = w.reshape(T, nb, gb, H, P)
        return jnp.einsum("tbgip,gh->bgpthi", w, eye).reshape(nb, gb * P, T * LANES)

    m2 = jnp.concatenate([place2(w_re), -place2(w_im)], 1).astype(BF16)

    def lanes(a):
        lead = a.shape[:-2]
        a = a.reshape(lead + (nb, gb * P))
        return jnp.moveaxis(a, -2, 0)

    a_re = lanes(aT_re)[:, None, :]
    a_im = lanes(aT_im)[:, None, :]
    p_re = lanes(pT_re)
    p_im = lanes(pT_im)
    a2_re = lanes(a2_re)[:, None, :]
    a2_im = lanes(a2_im)[:, None, :]
    return m1, m2, m3, a_re, a_im, p_re, p_im, a2_re, a2_im


def _mixout_kernel(*refs, glu):
    if glu:
        (a_ref, y_ref, x_ref, wglu_ref, bglu_ref, wo_ref, g_ref, lng_ref, lnb_ref,
         o_ref) = refs
        y = y_ref[...].astype(F32)
        s = y * (0.5 * (1.0 + jnp.tanh(math.sqrt(2.0 / math.pi)
                                       * (y + 0.044715 * (y * y * y)))))
        z = _dot(s.astype(BF16), wglu_ref[...]) + bglu_ref[...]
        s2 = (s * _sigmoid(z)).astype(BF16)
        wa = a_ref.shape[1]
        mix = _dot(a_ref[...], wo_ref[0:wa, :]) + _dot(s2, wo_ref[wa:, :])
    else:
        a_ref, x_ref, wo_ref, g_ref, lng_ref, lnb_ref, o_ref = refs
        mix = _dot(a_ref[...], wo_ref[...])
    r = DEEPNORM_ALPHA * x_ref[...] + g_ref[...] * mix
    o_ref[...] = _layer_norm(r, lng_ref[...], lnb_ref[...])


def _mixout(a, y, x, wglu, bglu, wo, g, lng, lnb):
    L, d = x.shape
    tm = 512
    glu = y is not None
    row = lambda i: (i, 0)
    fix = lambda i: (0, 0)
    vec = pl.BlockSpec((1, d), fix)
    if glu:
        ws = y.shape[1]
        args = (a, y, x, wglu, bglu, wo, g, lng, lnb)
        in_specs = [pl.BlockSpec((tm, a.shape[1]), row), pl.BlockSpec((tm, ws), row),
                    pl.BlockSpec((tm, d), row), pl.BlockSpec((ws, ws), fix),
                    pl.BlockSpec((1, ws), fix), pl.BlockSpec(wo.shape, fix), vec, vec, vec]
    else:
        args = (a, x, wo, g, lng, lnb)
        in_specs = [pl.BlockSpec((tm, a.shape[1]), row), pl.BlockSpec((tm, d), row),
                    pl.BlockSpec(wo.shape, fix), vec, vec, vec]
    return pl.pallas_call(
        functools.partial(_mixout_kernel, glu=glu),
        grid=(L // tm,),
        in_specs=in_specs,
        out_specs=pl.BlockSpec((tm, d), row),
        out_shape=jax.ShapeDtypeStruct((L, d), F32),
        compiler_params=_cparams(1, 48),
        name="mixout_glu" if glu else "mixout",
    )(*args)


def _ffn_kernel(x_ref, sc_ref, sh_ref, wg_ref, wu_ref, wd_ref, g_ref, lng_ref, lnb_ref,
                o_ref, h_sc, acc_sc):
    j = pl.program_id(1)

    @pl.when(j == 0)
    def _():
        h_sc[...] = (x_ref[...] * (1.0 + sc_ref[...]) + sh_ref[...]).astype(BF16)
        acc_sc[...] = jnp.zeros_like(acc_sc)

    h = h_sc[...]
    gt = _dot(h, wg_ref[...])
    ut = _dot(h, wu_ref[...])
    act = (gt * _sigmoid(gt) * ut).astype(BF16)
    acc_sc[...] += _dot(act, wd_ref[...])

    @pl.when(j == pl.num_programs(1) - 1)
    def _():
        r = DEEPNORM_ALPHA * x_ref[...] + g_ref[...] * acc_sc[...]
        o_ref[...] = _layer_norm(r, lng_ref[...], lnb_ref[...])


def _ffn(x, sc, sh, wg, wu, wd, g, lng, lnb):
    L, d = x.shape
    dff = wg.shape[1]
    tm, tf = 512, 512
    row = lambda i, j: (i, 0)
    vec = pl.BlockSpec((1, d), lambda i, j: (0, 0))
    return pl.pallas_call(
        _ffn_kernel,
        grid=(L // tm, dff // tf),
        in_specs=[pl.BlockSpec((tm, d), row), vec, vec,
                  pl.BlockSpec((d, tf), lambda i, j: (0, j)),
                  pl.BlockSpec((d, tf), lambda i, j: (0, j)),
                  pl.BlockSpec((tf, d), lambda i, j: (j, 0)),
                  vec, vec, vec],
        out_specs=pl.BlockSpec((tm, d), row),
        out_shape=jax.ShapeDtypeStruct((L, d), F32),
        scratch_shapes=[pltpu.VMEM((tm, d), BF16), pltpu.VMEM((tm, d), F32)],
        compiler_params=_cparams(2, 48),
        name="ffn",
    )(x, sc, sh, wg, wu, wd, g, lng, lnb)


def _inproj1_kernel(x_ref, sc_ref, sh_ref, pos_ref, freq_ref, slo_ref, shi_ref, w_ref,
                    o_ref, h_sc, cos_sc, sinlo_sc, sinhi_sc, *, nq, nk, scale):
    j = pl.program_id(1)
    tn = o_ref.shape[1]
    nchunk = tn // LANES

    @pl.when(j == 0)
    def _():
        h_sc[...] = (x_ref[...] * (1.0 + sc_ref[...]) + sh_ref[...]).astype(BF16)
        ang = pos_ref[...] * freq_ref[...]
        sn = jnp.sin(ang)
        cos_sc[...] = jnp.cos(ang)
        sinlo_sc[...] = sn * slo_ref[...]
        sinhi_sc[...] = sn * shi_ref[...]

    r = _dot(h_sc[...], w_ref[...])
    half = ROT_DIM // 2

    def rope(xc):
        return (xc * cos_sc[...]
                + pltpu.roll(xc, LANES - half, 1) * sinlo_sc[...]
                + pltpu.roll(xc, half, 1) * sinhi_sc[...])

    def emit(n_rot, mul):
        for c in range(nchunk):
            xc = r[:, c * LANES:(c + 1) * LANES]
            if c < n_rot:
                xc = rope(xc)
            if mul != 1.0:
                xc = xc * mul
            o_ref[:, c * LANES:(c + 1) * LANES] = xc.astype(BF16)

    @pl.when(j < nq)
    def _():
        emit(nchunk, scale)

    @pl.when(j >= nq)
    def _():
        emit(nk, 1.0)


def _inproj1(x, sc, sh, pos, freq, slo, shi, w, n_q_cols, n_k_cols):
    L, d = x.shape
    n = w.shape[1]
    tm, tn = 512, 512
    assert n_q_cols % tn == 0 and n - n_q_cols == tn
    kern = functools.partial(_inproj1_kernel, nq=n_q_cols // tn, nk=n_k_cols // LANES,
                             scale=1.0 / math.sqrt(SWA_HEAD_DIM))
    fix = lambda i, j: (0, 0)
    return pl.pallas_call(
        kern,
        grid=(L // tm, n // tn),
        in_specs=[pl.BlockSpec((tm, d), lambda i, j: (i, 0)),
                  pl.BlockSpec((1, d), fix), pl.BlockSpec((1, d), fix),
                  pl.BlockSpec((tm, 1), lambda i, j: (i, 0)),
                  pl.BlockSpec((1, LANES), fix), pl.BlockSpec((1, LANES), fix),
                  pl.BlockSpec((1, LANES), fix),
                  pl.BlockSpec((d, tn), lambda i, j: (0, j))],
        out_specs=pl.BlockSpec((tm, tn), lambda i, j: (i, j)),
        out_shape=jax.ShapeDtypeStruct((L, n), BF16),
        scratch_shapes=[pltpu.VMEM((tm, d), BF16), pltpu.VMEM((tm, LANES), F32),
                        pltpu.VMEM((tm, LANES), F32), pltpu.VMEM((tm, LANES), F32)],
        compiler_params=_cparams(2, 40),
        name="inproj1",
    )(x, sc, sh, pos, freq, slo, shi, w)


def _swa_kernel(sink_ref, q_ref, kc_ref, kp_ref, vc_ref, vp_ref, o_ref, *, n_kv):
    i = pl.program_id(0)
    W = WINDOW
    dh = SWA_HEAD_DIM
    qi = lax.broadcasted_iota(jnp.int32, (W, W), 0)
    kj = lax.broadcasted_iota(jnp.int32, (W, W), 1)
    mask_prev = jnp.logical_and(kj > qi, i > 0)
    mask_cur = kj <= qi
    for kvh in range(n_kv):
        kc = kc_ref[:, kvh * dh:(kvh + 1) * dh]
        kp = kp_ref[:, kvh * dh:(kvh + 1) * dh]
        vc = vc_ref[:, kvh * dh:(kvh + 1) * dh]
        vp = vp_ref[:, kvh * dh:(kvh + 1) * dh]
        for gp in range(SWA_Q_PER_KV // 2):
            outs = []
            for g in (2 * gp, 2 * gp + 1):
                hq = kvh * SWA_Q_PER_KV + g
                q = q_ref[:, hq * dh:(hq + 1) * dh]
                sp = jnp.where(mask_prev, _dot_nt(q, kp), NEG_BIG)
                sc = jnp.where(mask_cur, _dot_nt(q, kc), NEG_BIG)
                sink = sink_ref[hq]
                m = jnp.maximum(jnp.maximum(jnp.max(sp, axis=1, keepdims=True),
                                            jnp.max(sc, axis=1, keepdims=True)), sink)
                pp = jnp.exp(sp - m)
                pc = jnp.exp(sc - m)
                den = (jnp.sum(pp, axis=1, keepdims=True)
                       + jnp.sum(pc, axis=1, keepdims=True) + jnp.exp(sink - m))
                o = _dot(pp.astype(BF16), vp) + _dot(pc.astype(BF16), vc)
                outs.append(o / den)
            lo = 2 * gp * dh + kvh * SWA_Q_PER_KV * dh
            o_ref[:, lo:lo + 2 * dh] = jnp.concatenate(outs, axis=1).astype(BF16)


def _swa(qkv, sinks, n_q_cols, n_kv):
    L = qkv.shape[0]
    W = WINDOW
    kw = n_kv * SWA_HEAD_DIM
    kb = n_q_cols // kw
    prev = lambda i, s: (jnp.maximum(i - 1, 0), kb)
    prev_v = lambda i, s: (jnp.maximum(i - 1, 0), kb + 1)
    grid_spec = pltpu.PrefetchScalarGridSpec(
        num_scalar_prefetch=1,
        grid=(L // W,),
        in_specs=[pl.BlockSpec((W, n_q_cols), lambda i, s: (i, 0)),
                  pl.BlockSpec((W, kw), lambda i, s: (i, kb)),
                  pl.BlockSpec((W, kw), prev),
                  pl.BlockSpec((W, kw), lambda i, s: (i, kb + 1)),
                  pl.BlockSpec((W, kw), prev_v)],
        out_specs=pl.BlockSpec((W, n_q_cols), lambda i, s: (i, 0)),
    )
    return pl.pallas_call(
        functools.partial(_swa_kernel, n_kv=n_kv),
        grid_spec=grid_spec,
        out_shape=jax.ShapeDtypeStruct((L, n_q_cols), BF16),
        compiler_params=_cparams(1, 32),
        name="swa",
    )(sinks, qkv, qkv, qkv, qkv, qkv)


def kernel(x, c, positions, w_in_ab, b_forget, ssm_lambda_re, ssm_lambda_im, ssm_log_dt, ssm_b_re, ssm_b_im, ssm_c_re, ssm_c_im, ssm_d, w_glu, b_glu, w_out_ab, w_in_c, attn_sinks, w_out_c, w_ada, b_ada, ln_mix_g, ln_mix_b, ln_ffn_g, ln_ffn_b, w_ffn_gate, w_ffn_up, w_ffn_down):
    B, L, D = x.shape
    assert B == 1
    x2 = x.reshape(L, D)
    mod = _ada(c, w_ada, b_ada)

    def mods(layer):
        return [mod[layer, :, k * D:(k + 1) * D] for k in range(6)]

    def vec(a):
        return a.reshape(1, -1)

    sh1, sc1, g1, sh2, sc2, g2 = mods(0)
    W = FOX_WIDTH
    w_in = w_in_ab[0]
    w_main = jnp.concatenate([w_in[:, :3 * W], w_in[:, 3 * W + FOX_HEADS:]], 1).astype(BF16)
    w_f = jnp.pad(w_in[:, 3 * W:3 * W + FOX_HEADS], ((0, 0), (0, LANES - FOX_HEADS))).astype(BF16)
    b_f = jnp.pad(b_forget[0], (0, LANES - FOX_HEADS)).reshape(1, LANES)
    qkvu, fcum = _inproj0(x2, sc1, sh1, w_main, w_f, b_f)

    tq = 512
    f_heads = fcum[:, :FOX_HEADS].T
    a_out = _fox(qkvu, f_heads.reshape(FOX_HEADS, L, 1),
                 f_heads.reshape(FOX_HEADS, L // tq, 1, tq))

    T = SSM_CHUNK
    nrow = L // T
    nsub = nrow // SUBLANES
    width = qkvu.shape[1] - 3 * W
    u = qkvu[:, 3 * W:]
    u_s = u.reshape(SUBLANES, nsub, T, width).transpose(2, 1, 0, 3).reshape(T, nrow, width)
    tables = _ssm_tables(ssm_lambda_re[0], ssm_lambda_im[0], ssm_log_dt[0], ssm_b_re[0],
                         ssm_b_im[0], ssm_c_re[0], ssm_c_im[0], ssm_d[0], nsub)
    y_s = _ssm(u_s, *tables)
    y = y_s.reshape(T, nsub, SUBLANES, width).transpose(2, 1, 0, 3).reshape(L, width)

    x2 = _mixout(a_out, y, x2, w_glu[0].astype(BF16), vec(b_glu[0]), w_out_ab[0].astype(BF16),
                 g1, vec(ln_mix_g[0]), vec(ln_mix_b[0]))
    x2 = _ffn(x2, sc2, sh2, w_ffn_gate[0].astype(BF16), w_ffn_up[0].astype(BF16),
              w_ffn_down[0].astype(BF16), g2, vec(ln_ffn_g[0]), vec(ln_ffn_b[0]))

    sh1, sc1, g1, sh2, sc2, g2 = mods(1)
    n_q = w_out_c.shape[1]
    n_kv = (w_in_c.shape[2] - n_q) // (2 * SWA_HEAD_DIM)
    half = ROT_DIM // 2
    inv_freq = jnp.power(jnp.float32(ROPE_THETA), -jnp.arange(half, dtype=F32) * (2.0 / ROT_DIM))
    dlane = jnp.arange(LANES) % SWA_HEAD_DIM
    freq = jnp.where(dlane < ROT_DIM, inv_freq[dlane % half], 0.0).reshape(1, LANES)
    slo = jnp.where(dlane < half, -1.0, 0.0).astype(F32).reshape(1, LANES)
    shi = jnp.where((dlane >= half) & (dlane < ROT_DIM), 1.0, 0.0).astype(F32).reshape(1, LANES)
    pos = positions.reshape(L, 1).astype(F32)
    qkv = _inproj1(x2, sc1, sh1, pos, freq, slo, shi, w_in_c[0].astype(BF16),
                   n_q, n_kv * SWA_HEAD_DIM)
    a1 = _swa(qkv, attn_sinks[0], n_q, n_kv)
    x2 = _mixout(a1, None, x2, None, None, w_out_c[0].astype(BF16), g1,
                 vec(ln_mix_g[1]), vec(ln_mix_b[1]))
    x2 = _ffn(x2, sc2, sh2, w_ffn_gate[1].astype(BF16), w_ffn_up[1].astype(BF16),
              w_ffn_down[1].astype(BF16), g2, vec(ln_ffn_g[1]), vec(ln_ffn_b[1]))
    return x2.reshape(B, L, D)
```

```python
import functools
import math

import jax
import jax.numpy as jnp
from jax import lax
from jax.experimental import pallas as pl
from jax.experimental.pallas import tpu as pltpu

F32 = jnp.float32
BF16 = jnp.bfloat16

FOX_HEADS = 8
FOX_HEAD_DIM = 128
FOX_WIDTH = FOX_HEADS * FOX_HEAD_DIM
SSM_GROUP = 16
SSM_STATE = 64
SWA_HEAD_DIM = 64
SWA_Q_PER_KV = 8
WINDOW = 128
ROT_DIM = SWA_HEAD_DIM // 4
ROPE_THETA = 500000.0
DEPTH = 2
DEEPNORM_ALPHA = (2 * DEPTH) ** 0.25
LN_EPS = 1e-5

LANES = 128
SUBLANES = 8
NEG_BIG = -1e30

SSM_CHUNK = 16
SSM_BLOCK_GROUPS = LANES // SSM_GROUP


def _cparams(n_axes, vmem_mb):
    return pltpu.CompilerParams(
        dimension_semantics=("arbitrary",) * n_axes,
        vmem_limit_bytes=vmem_mb * 1024 * 1024)


def _sigmoid(x):
    return 1.0 / (1.0 + jnp.exp(-x))


def _layer_norm(r, g, b):
    mu = jnp.mean(r, axis=-1, keepdims=True)
    d = r - mu
    var = jnp.mean(d * d, axis=-1, keepdims=True)
    return d * lax.rsqrt(var + LN_EPS) * g + b


def _dot(a, b):
    return jnp.dot(a, b, preferred_element_type=F32)


def _dot_nt(a, b):
    return lax.dot_general(a, b, (((1,), (1,)), ((), ())), preferred_element_type=F32)


def _ada_kernel(c_ref, w_ref, b_ref, o_ref):
    c = c_ref[...]
    sc = c * _sigmoid(c)
    o_ref[0] = jnp.dot(sc, w_ref[0], precision=lax.Precision.HIGHEST,
                       preferred_element_type=F32) + b_ref[0]


def _ada(c, w_ada, b_ada):
    depth, d, n = w_ada.shape
    tn = 1536
    c8 = jnp.broadcast_to(c, (SUBLANES, d))
    out = pl.pallas_call(
        _ada_kernel,
        grid=(depth, n // tn),
        in_specs=[pl.BlockSpec((SUBLANES, d), lambda l, j: (0, 0)),
                  pl.BlockSpec((1, d, tn), lambda l, j: (l, 0, j)),
                  pl.BlockSpec((1, 1, tn), lambda l, j: (l, 0, j))],
        out_specs=pl.BlockSpec((1, SUBLANES, tn), lambda l, j: (l, 0, j)),
        out_shape=jax.ShapeDtypeStruct((depth, SUBLANES, n), F32),
        compiler_params=_cparams(2, 40),
        name="ada",
    )(c8, w_ada, b_ada.reshape(depth, 1, n))
    return out[:, 0:1, :]


N_SPLIT = 3
ONES_ROWS = 16
LOG2E = 1.0 / math.log(2.0)


def _inproj0_kernel(x_ref, sc_ref, sh_ref, w_ref, wvt_ref, wf_ref, bf_ref, pq_ref, pk_ref,
                    cq_ref, ck_ref, qa_ref, ka_ref, u_ref, vt_ref, h_sc, carry_sc,
                    *, tm, scale):
    i = pl.program_id(0)
    j = pl.program_id(1)
    hd = FOX_HEAD_DIM

    def put_heads(dst_ref, val, off):
        for h in range(FOX_HEADS):
            dst_ref[:, 2 * h * hd + off:2 * h * hd + off + hd] = (
                val[:, h * hd:(h + 1) * hd].astype(BF16))

    @pl.when(j == 0)
    def _():
        hb = (x_ref[...] * (1.0 + sc_ref[...]) + sh_ref[...]).astype(BF16)
        h_sc[...] = hb
        z = _dot(hb, wf_ref[...]) + bf_ref[...]
        lf = jnp.minimum(z, 0.0) - jnp.log1p(jnp.exp(-jnp.abs(z)))

        def split(v):
            p1 = v.astype(BF16)
            r1 = v - p1.astype(F32)
            p2 = r1.astype(BF16)
            return p1, p2, (r1 - p2.astype(F32)).astype(BF16)

        row = lax.broadcasted_iota(jnp.int32, (tm, tm), 0)
        col = lax.broadcasted_iota(jnp.int32, (tm, tm), 1)
        tri = (col <= row).astype(BF16)
        cum = sum(_dot(tri, p) for p in split(lf))

        @pl.when(i == 0)
        def _():
            carry_sc[...] = jnp.zeros_like(carry_sc)

        f = cum + carry_sc[...]
        carry_sc[...] = f[tm - 1:tm, :]
        fp = jnp.concatenate(split(f * LOG2E), axis=1)
        put_heads(qa_ref, _dot(fp, pq_ref[...]) + cq_ref[...], hd)
        put_heads(ka_ref, _dot(fp, pk_ref[...]) + ck_ref[...], hd)
        put_heads(qa_ref, _dot(hb, w_ref[...]) * scale, 0)

    @pl.when(j == 1)
    def _():
        put_heads(ka_ref, _dot(h_sc[...], w_ref[...]), 0)

    @pl.when(j == 2)
    def _():
        u_ref[...] = _dot(h_sc[...], w_ref[...]).astype(BF16)

    @pl.when(j == 3)
    def _():
        vt = _dot_nt(wvt_ref[...], h_sc[...])
        vt_ref[:, :, 0:hd, :] = vt.astype(BF16).reshape(FOX_HEADS, 1, hd, tm)
        vt_ref[:, :, hd:, :] = jnp.ones((FOX_HEADS, 1, ONES_ROWS, tm), BF16)


def _inproj0(x, sc, sh, w_qku, w_vt, wf, bf, tm):
    L, d = x.shape
    W = FOX_WIDTH
    hd = FOX_HEAD_DIM
    piece = jnp.arange(N_SPLIT)
    head = jnp.arange(FOX_HEADS)
    rows = (piece[:, None] * LANES + head[None, :]).reshape(-1)
    cols_k = (head[None, :] * hd + piece[:, None]).reshape(-1)
    cols_q = cols_k + N_SPLIT
    pq = jnp.zeros((N_SPLIT * LANES, W), F32).at[rows, cols_q].set(1.0).astype(BF16)
    pk = jnp.zeros((N_SPLIT * LANES, W), F32).at[rows, cols_k].set(-1.0).astype(BF16)
    lane = jnp.arange(W) % hd
    cq = (lane < N_SPLIT).astype(F32).reshape(1, W)
    ck = ((lane >= N_SPLIT) & (lane < 2 * N_SPLIT)).astype(F32).reshape(1, W)
    kern = functools.partial(_inproj0_kernel, tm=tm, scale=LOG2E / math.sqrt(hd))
    hv = hd + ONES_ROWS
    fix = lambda i, j: (0, 0)
    row = lambda i, j: (i, 0)
    return pl.pallas_call(
        kern,
        grid=(L // tm, 4),
        in_specs=[pl.BlockSpec((tm, d), row),
                  pl.BlockSpec((1, d), fix),
                  pl.BlockSpec((1, d), fix),
                  pl.BlockSpec((d, W), lambda i, j: (0, jnp.minimum(j, 2))),
                  pl.BlockSpec((W, d), fix),
                  pl.BlockSpec((d, LANES), fix),
                  pl.BlockSpec((1, LANES), fix),
                  pl.BlockSpec((N_SPLIT * LANES, W), fix),
                  pl.BlockSpec((N_SPLIT * LANES, W), fix),
                  pl.BlockSpec((1, W), fix),
                  pl.BlockSpec((1, W), fix)],
        out_specs=[pl.BlockSpec((tm, 2 * W), row),
                   pl.BlockSpec((tm, 2 * W), row),
                   pl.BlockSpec((tm, W), row),
                   pl.BlockSpec((FOX_HEADS, 1, hv, tm), lambda i, j: (0, i, 0, 0))],
        out_shape=[jax.ShapeDtypeStruct((L, 2 * W), BF16),
                   jax.ShapeDtypeStruct((L, 2 * W), BF16),
                   jax.ShapeDtypeStruct((L, W), BF16),
                   jax.ShapeDtypeStruct((FOX_HEADS, L // tm, hv, tm), BF16)],
        scratch_shapes=[pltpu.VMEM((tm, d), BF16), pltpu.VMEM((1, LANES), F32)],
        compiler_params=_cparams(2, 48),
        name="inproj0",
    )(x, sc, sh, w_qku, w_vt, wf, bf, pq, pk, cq, ck)


M_INIT = -1e29


def _fox_kernel(qa_ref, ka_ref, vt_ref, o_ref, s0, s1, p0, p1, al0, al1, mb0, mb1, m_sc,
                acc_sc, mask_sc, *, tq):
    first = jnp.logical_and(pl.program_id(0) == 0, pl.program_id(1) == 0)
    i = pl.program_id(1)

    @pl.when(first)
    def _():
        key = lax.broadcasted_iota(jnp.int32, (tq, tq), 0)
        qry = lax.broadcasted_iota(jnp.int32, (tq, tq), 1)
        mask_sc[0] = jnp.zeros((tq, tq), F32)
        mask_sc[1] = jnp.where(key <= qry, 0.0, NEG_BIG)
        mask_sc[2] = jnp.full((tq, tq), NEG_BIG, F32)

    q = qa_ref[...]
    hd = o_ref.shape[1]
    s1[...] = jnp.full_like(s1, NEG_BIG)
    mb1[...] = jnp.full_like(mb1, NEG_BIG)
    p0[...] = jnp.zeros_like(p0)
    al0[...] = jnp.ones_like(al0)
    m_sc[...] = jnp.full_like(m_sc, M_INIT)
    acc_sc[...] = jnp.zeros_like(acc_sc)

    def scores(t, s_ref, mb_ref, masked):
        start = pl.multiple_of(jnp.minimum(t, i) * tq, tq)
        s = _dot_nt(ka_ref[pl.ds(start, tq), :], q)
        if masked:
            s = s + mask_sc[jnp.clip(t - i + 1, 0, 2)]
        s_ref[...] = s
        mb_ref[...] = jnp.max(s, axis=0, keepdims=True)

    def softmax(s_ref, mb_ref, p_ref, al_ref):
        m_prev = m_sc[...]
        m_new = jnp.maximum(m_prev, mb_ref[...])
        m_sc[...] = m_new
        al_ref[...] = jnp.exp2(m_prev - m_new)
        p_ref[...] = jnp.exp2(s_ref[...] - m_new).astype(BF16)

    def values(t, p_ref, al_ref):
        blk = jnp.clip(t, 0, i)
        acc_sc[...] = al_ref[...] * acc_sc[...] + _dot(vt_ref[0, blk], p_ref[...])

    def pair(u, carry, masked):
        t = 2 * u
        scores(t, s0, mb0, masked)
        softmax(s1, mb1, p1, al1)
        values(t - 2, p0, al0)
        scores(t + 1, s1, mb1, masked)
        softmax(s0, mb0, p0, al0)
        values(t - 1, p1, al1)
        return carry

    n_plain = i // 2
    lax.fori_loop(0, n_plain, functools.partial(pair, masked=False), 0)
    lax.fori_loop(n_plain, (i + 4) // 2, functools.partial(pair, masked=True), 0)
    acc = acc_sc[...]
    o_ref[...] = (acc[0:hd] / acc[hd:hd + 1]).T.astype(BF16)


def _fox(qa, ka, vt, tq):
    L = qa.shape[0]
    hd = FOX_HEAD_DIM
    hv = hd + ONES_ROWS
    assert vt.shape == (FOX_HEADS, L // tq, hv, tq)
    row = pltpu.VMEM((1, tq), F32)
    return pl.pallas_call(
        functools.partial(_fox_kernel, tq=tq),
        grid=(FOX_HEADS, L // tq),
        in_specs=[pl.BlockSpec((tq, 2 * hd), lambda h, i: (i, h)),
                  pl.BlockSpec((L, 2 * hd), lambda h, i: (0, h)),
                  pl.BlockSpec((1, L // tq, hv, tq), lambda h, i: (h, 0, 0, 0))],
        out_specs=pl.BlockSpec((tq, hd), lambda h, i: (i, h)),
        out_shape=jax.ShapeDtypeStruct((L, FOX_WIDTH), BF16),
        scratch_shapes=[pltpu.VMEM((tq, tq), F32), pltpu.VMEM((tq, tq), F32),
                        pltpu.VMEM((tq, tq), BF16), pltpu.VMEM((tq, tq), BF16),
                        row, row, row, row, row,
                        pltpu.VMEM((hv, tq), F32), pltpu.VMEM((3, tq, tq), F32)],
        compiler_params=_cparams(2, 48),
        name="fox",
    )(qa, ka, vt)


def _taps_kernel(cre_ref, cim_ref, pre_ref, pim_ref, bre_ref, bim_ref, o_ref):
    T, H, P = SSM_CHUNK, SSM_GROUP, SSM_STATE
    for g in range(SSM_BLOCK_GROUPS):
        cre = cre_ref[g][None]
        cim = cim_ref[g][None]
        pre = pre_ref[g][:, None, :]
        pim = pim_ref[g][:, None, :]
        wre = (cre * pre - cim * pim).reshape(T * H, P)
        wim = (cre * pim + cim * pre).reshape(T * H, P)
        k = (jnp.dot(wre, bre_ref[g], precision=lax.Precision.HIGHEST,
                     preferred_element_type=F32)
             - jnp.dot(wim, bim_ref[g], precision=lax.Precision.HIGHEST,
                       preferred_element_type=F32))
        o_ref[g] = k


def _taps(c_re, c_im, pw_re, pw_im, bb_re, bb_im):
    G = c_re.shape[0]
    gb = SSM_BLOCK_GROUPS
    T, H, P = SSM_CHUNK, SSM_GROUP, SSM_STATE
    return pl.pallas_call(
        _taps_kernel,
        grid=(G // gb,),
        in_specs=[pl.BlockSpec((gb, H, P), lambda b: (b, 0, 0)),
                  pl.BlockSpec((gb, H, P), lambda b: (b, 0, 0)),
                  pl.BlockSpec((gb, T, P), lambda b: (b, 0, 0)),
                  pl.BlockSpec((gb, T, P), lambda b: (b, 0, 0)),
                  pl.BlockSpec((gb, P, H), lambda b: (b, 0, 0)),
                  pl.BlockSpec((gb, P, H), lambda b: (b, 0, 0))],
        out_specs=pl.BlockSpec((gb, T * H, H), lambda b: (b, 0, 0)),
        out_shape=jax.ShapeDtypeStruct((G, T * H, H), F32),
        compiler_params=_cparams(1, 32),
        name="taps",
    )(c_re, c_im, pw_re, pw_im, bb_re, bb_im)


def _ssm_kernel(u_ref, m1_ref, m2_ref, m3_ref, ar_ref, ai_ref, pr_ref, pi_ref,
                a2r_ref, a2i_ref, o_ref, ucat_sc, s_sc, x_sc, *, nrow):
    tp = pl.program_id(1)
    T = SSM_CHUNK
    half = x_sc.shape[1] // 2
    ngrp = nrow // SUBLANES

    @pl.when(tp == 0)
    def _():
        for s in range(T):
            ucat_sc[:, s * LANES:(s + 1) * LANES] = u_ref[s]
        s_sc[...] = _dot(ucat_sc[...], m3_ref[0])
        ar = ar_ref[0]
        ai = ai_ref[0]

        def step(r, carry):
            xr, xi = carry
            row = pl.multiple_of(r * SUBLANES, SUBLANES)
            x_sc[pl.ds(row, SUBLANES), 0:half] = xr
            x_sc[pl.ds(row, SUBLANES), half:2 * half] = xi
            sr = s_sc[pl.ds(row, SUBLANES), 0:half]
            si = s_sc[pl.ds(row, SUBLANES), half:2 * half]
            return ar * xr - ai * xi + sr, ar * xi + ai * xr + si

        zero = jnp.zeros((SUBLANES, half), F32)
        er, ei = lax.fori_loop(0, ngrp, step, (zero, zero))

        a2r = a2r_ref[0]
        a2i = a2i_ref[0]
        sub = lax.broadcasted_iota(jnp.int32, (SUBLANES, half), 0)
        zr = jnp.zeros((1, half), F32)
        zi = jnp.zeros((1, half), F32)
        zr_all = zero
        zi_all = zero
        for jj in range(1, SUBLANES):
            nzr = a2r * zr - a2i * zi + er[jj - 1:jj]
            nzi = a2r * zi + a2i * zr + ei[jj - 1:jj]
            zr, zi = nzr, nzi
            zr_all = jnp.where(sub == jj, zr, zr_all)
            zi_all = jnp.where(sub == jj, zi, zi_all)

        def fix(r, carry):
            row = pl.multiple_of(r * SUBLANES, SUBLANES)
            pr = pr_ref[0, pl.ds(r, 1), :]
            pi = pi_ref[0, pl.ds(r, 1), :]
            x_sc[pl.ds(row, SUBLANES), 0:half] = (
                x_sc[pl.ds(row, SUBLANES), 0:half] + pr * zr_all - pi * zi_all)
            x_sc[pl.ds(row, SUBLANES), half:2 * half] = (
                x_sc[pl.ds(row, SUBLANES), half:2 * half] + pr * zi_all + pi * zr_all)
            return carry

        lax.fori_loop(0, ngrp, fix, 0)

    y = _dot(ucat_sc[...], m1_ref[0]) + _dot(x_sc[...].astype(BF16), m2_ref[0])
    o_ref[0] = y[:, 0:LANES].astype(BF16)
    o_ref[1] = y[:, LANES:2 * LANES].astype(BF16)


def _ssm(u_s, m1, m2, m3, a_re, a_im, p_re, p_im, a2_re, a2_im):
    T, nrow, width = u_s.shape
    nb = width // LANES
    nstate = m3.shape[2]
    half = nstate // 2
    ngrp = nrow // SUBLANES
    kern = functools.partial(_ssm_kernel, nrow=nrow)
    return pl.pallas_call(
        kern,
        grid=(nb, T // 2),
        in_specs=[pl.BlockSpec((T, nrow, LANES), lambda b, t: (0, 0, b)),
                  pl.BlockSpec((1, T * LANES, 2 * LANES), lambda b, t: (b, 0, t)),
                  pl.BlockSpec((1, nstate, 2 * LANES), lambda b, t: (b, 0, t)),
                  pl.BlockSpec((1, T * LANES, nstate), lambda b, t: (b, 0, 0)),
                  pl.BlockSpec((1, 1, half), lambda b, t: (b, 0, 0)),
                  pl.BlockSpec((1, 1, half), lambda b, t: (b, 0, 0)),
                  pl.BlockSpec((1, ngrp, half), lambda b, t: (b, 0, 0)),
                  pl.BlockSpec((1, ngrp, half), lambda b, t: (b, 0, 0)),
                  pl.BlockSpec((1, 1, half), lambda b, t: (b, 0, 0)),
                  pl.BlockSpec((1, 1, half), lambda b, t: (b, 0, 0))],
        out_specs=pl.BlockSpec((2, nrow, LANES), lambda b, t: (t, 0, b)),
        out_shape=jax.ShapeDtypeStruct((T, nrow, width), BF16),
        scratch_shapes=[pltpu.VMEM((nrow, T * LANES), BF16),
                        pltpu.VMEM((nrow, nstate), F32),
                        pltpu.VMEM((nrow, nstate), F32)],
        compiler_params=_cparams(2, 48),
        name="ssm",
    )(u_s, m1, m2, m3, a_re, a_im, p_re, p_im, a2_re, a2_im)


def _ssm_tables(lam_re, lam_im, log_dt, b_re, b_im, c_re, c_im, d_skip, ngrp):
    G, P = lam_re.shape
    H = SSM_GROUP
    T = SSM_CHUNK
    gb = SSM_BLOCK_GROUPS
    nb = G // gb
    dt = jnp.exp(log_dt)[:, None]
    mag = jnp.exp(lam_re * dt)
    lb_re = mag * jnp.cos(lam_im * dt)
    lb_im = mag * jnp.sin(lam_im * dt)
    den = lam_re * lam_re + lam_im * lam_im
    nr = lb_re - 1.0
    q_re = (nr * lam_re + lb_im * lam_im) / den
    q_im = (lb_im * lam_re - nr * lam_im) / den
    bb_re = q_re[..., None] * b_re - q_im[..., None] * b_im
    bb_im = q_re[..., None] * b_im + q_im[..., None] * b_re

    def cmul(ar, ai, br, bi):
        return ar * br - ai * bi, ar * bi + ai * br

    def powers(ar, ai, n):
        pr, pi = jnp.ones_like(ar)[None], jnp.zeros_like(ai)[None]
        sr, si = ar, ai
        while pr.shape[0] < n:
            nr_, ni_ = cmul(pr, pi, sr[None], si[None])
            pr = jnp.concatenate([pr, nr_], 0)
            pi = jnp.concatenate([pi, ni_], 0)
            sr, si = cmul(sr, si, sr, si)
        return pr[:n], pi[:n], sr, si

    pw_re, pw_im, _, _ = powers(lb_re, lb_im, 2 * T)
    aT_re, aT_im = pw_re[T], pw_im[T]
    pT_re, pT_im, a2_re, a2_im = powers(aT_re, aT_im, ngrp)

    taps = _taps(c_re, c_im,
                 jnp.transpose(pw_re[:T], (1, 0, 2)), jnp.transpose(pw_im[:T], (1, 0, 2)),
                 bb_re, bb_im)
    kd = taps.reshape(G, T, H, H)
    kd = kd.at[:, 0].add(d_skip[:, :, None] * jnp.eye(H, dtype=F32))
    kpad = jnp.concatenate([kd, jnp.zeros((G, 1, H, H), F32)], 1)
    tt = jnp.arange(T)
    idx = jnp.where(tt[None, :] >= tt[:, None], tt[None, :] - tt[:, None], T)
    ktoe = kpad[:, idx]
    eye = jnp.eye(gb, dtype=F32)
    m1 = jnp.einsum("bgstij,gh->bsgjthi", ktoe.reshape(nb, gb, T, T, H, H), eye)
    m1 = m1.reshape(nb, T * LANES, T * LANES).astype(BF16)

    er = pw_re[:T][::-1]
    ei = pw_im[:T][::-1]
    v_re = er[..., None] * bb_re[None] - ei[..., None] * bb_im[None]
    v_im = er[..., None] * bb_im[None] + ei[..., None] * bb_re[None]

    def place3(v):
        v = v.reshape(T, nb, gb, P, H)
        return jnp.einsum("sbgpj,gh->bsgjhp", v, eye).reshape(nb, T * LANES, gb * P)

    m3 = jnp.concatenate([place3(v_re), place3(v_im)], -1).astype(BF16)

    fr = pw_re[1:T + 1]
    fi = pw_im[1:T + 1]
    w_re = c_re[None] * fr[:, :, None, :] - c_im[None] * fi[:, :, None, :]
    w_im = c_re[None] * fi[:, :, None, :] + c_im[None] * fr[:, :, None, :]

    def place2(w):
        w = w.reshape(T, nb, gb, H, P)
        return jnp.einsum("tbgip,gh->bgpthi", w, eye).reshape(nb, gb * P, T * LANES)

    m2 = jnp.concatenate([place2(w_re), -place2(w_im)], 1).astype(BF16)

    def lanes(a):
        lead = a.shape[:-2]
        a = a.reshape(lead + (nb, gb * P))
        return jnp.moveaxis(a, -2, 0)

    a_re = lanes(aT_re)[:, None, :]
    a_im = lanes(aT_im)[:, None, :]
    p_re = lanes(pT_re)
    p_im = lanes(pT_im)
    a2_re = lanes(a2_re)[:, None, :]
    a2_im = lanes(a2_im)[:, None, :]
    return m1, m2, m3, a_re, a_im, p_re, p_im, a2_re, a2_im


def _mixout_kernel(*refs, glu):
    if glu:
        (a_ref, y_ref, x_ref, wglu_ref, bglu_ref, wo_ref, g_ref, lng_ref, lnb_ref,
         o_ref) = refs
        y = y_ref[...].astype(F32)
        s = y * (0.5 * (1.0 + jnp.tanh(math.sqrt(2.0 / math.pi)
                                       * (y + 0.044715 * (y * y * y)))))
        z = _dot(s.astype(BF16), wglu_ref[...]) + bglu_ref[...]
        s2 = (s * _sigmoid(z)).astype(BF16)
        wa = a_ref.shape[1]
        mix = _dot(a_ref[...], wo_ref[0:wa, :]) + _dot(s2, wo_ref[wa:, :])
    else:
        a_ref, x_ref, wo_ref, g_ref, lng_ref, lnb_ref, o_ref = refs
        mix = _dot(a_ref[...], wo_ref[...])
    r = DEEPNORM_ALPHA * x_ref[...] + g_ref[...] * mix
    o_ref[...] = _layer_norm(r, lng_ref[...], lnb_ref[...])


def _mixout(a, y, x, wglu, bglu, wo, g, lng, lnb):
    L, d = x.shape
    tm = 512
    glu = y is not None
    row = lambda i: (i, 0)
    fix = lambda i: (0, 0)
    vec = pl.BlockSpec((1, d), fix)
    if glu:
        ws = y.shape[1]
        args = (a, y, x, wglu, bglu, wo, g, lng, lnb)
        in_specs = [pl.BlockSpec((tm, a.shape[1]), row), pl.BlockSpec((tm, ws), row),
                    pl.BlockSpec((tm, d), row), pl.BlockSpec((ws, ws), fix),
                    pl.BlockSpec((1, ws), fix), pl.BlockSpec(wo.shape, fix), vec, vec, vec]
    else:
        args = (a, x, wo, g, lng, lnb)
        in_specs = [pl.BlockSpec((tm, a.shape[1]), row), pl.BlockSpec((tm, d), row),
                    pl.BlockSpec(wo.shape, fix), vec, vec, vec]
    return pl.pallas_call(
        functools.partial(_mixout_kernel, glu=glu),
        grid=(L // tm,),
        in_specs=in_specs,
        out_specs=pl.BlockSpec((tm, d), row),
        out_shape=jax.ShapeDtypeStruct((L, d), F32),
        compiler_params=_cparams(1, 48),
        name="mixout_glu" if glu else "mixout",
    )(*args)


def _ffn_kernel(x_ref, sc_ref, sh_ref, wg_ref, wu_ref, wd_ref, g_ref, lng_ref, lnb_ref,
                o_ref, h_sc, acc_sc):
    j = pl.program_id(1)

    @pl.when(j == 0)
    def _():
        h_sc[...] = (x_ref[...] * (1.0 + sc_ref[...]) + sh_ref[...]).astype(BF16)
        acc_sc[...] = jnp.zeros_like(acc_sc)

    h = h_sc[...]
    gt = _dot(h, wg_ref[...])
    ut = _dot(h, wu_ref[...])
    act = (gt * _sigmoid(gt) * ut).astype(BF16)
    acc_sc[...] += _dot(act, wd_ref[...])

    @pl.when(j == pl.num_programs(1) - 1)
    def _():
        r = DEEPNORM_ALPHA * x_ref[...] + g_ref[...] * acc_sc[...]
        o_ref[...] = _layer_norm(r, lng_ref[...], lnb_ref[...])


def _ffn(x, sc, sh, wg, wu, wd, g, lng, lnb):
    L, d = x.shape
    dff = wg.shape[1]
    tm, tf = 512, 512
    row = lambda i, j: (i, 0)
    vec = pl.BlockSpec((1, d), lambda i, j: (0, 0))
    return pl.pallas_call(
        _ffn_kernel,
        grid=(L // tm, dff // tf),
        in_specs=[pl.BlockSpec((tm, d), row), vec, vec,
                  pl.BlockSpec((d, tf), lambda i, j: (0, j)),
                  pl.BlockSpec((d, tf), lambda i, j: (0, j)),
                  pl.BlockSpec((tf, d), lambda i, j: (j, 0)),
                  vec, vec, vec],
        out_specs=pl.BlockSpec((tm, d), row),
        out_shape=jax.ShapeDtypeStruct((L, d), F32),
        scratch_shapes=[pltpu.VMEM((tm, d), BF16), pltpu.VMEM((tm, d), F32)],
        compiler_params=_cparams(2, 48),
        name="ffn",
    )(x, sc, sh, wg, wu, wd, g, lng, lnb)


def _inproj1_kernel(x_ref, sc_ref, sh_ref, pos_ref, freq_ref, slo_ref, shi_ref, w_ref,
                    o_ref, h_sc, cos_sc, sinlo_sc, sinhi_sc, *, nq, nk, scale):
    j = pl.program_id(1)
    tn = o_ref.shape[1]
    nchunk = tn // LANES

    @pl.when(j == 0)
    def _():
        h_sc[...] = (x_ref[...] * (1.0 + sc_ref[...]) + sh_ref[...]).astype(BF16)
        ang = pos_ref[...] * freq_ref[...]
        sn = jnp.sin(ang)
        cos_sc[...] = jnp.cos(ang)
        sinlo_sc[...] = sn * slo_ref[...]
        sinhi_sc[...] = sn * shi_ref[...]

    r = _dot(h_sc[...], w_ref[...])
    half = ROT_DIM // 2

    def rope(xc):
        return (xc * cos_sc[...]
                + pltpu.roll(xc, LANES - half, 1) * sinlo_sc[...]
                + pltpu.roll(xc, half, 1) * sinhi_sc[...])

    def emit(n_rot, mul):
        for c in range(nchunk):
            xc = r[:, c * LANES:(c + 1) * LANES]
            if c < n_rot:
                xc = rope(xc)
            if mul != 1.0:
                xc = xc * mul
            o_ref[:, c * LANES:(c + 1) * LANES] = xc.astype(BF16)

    @pl.when(j < nq)
    def _():
        emit(nchunk, scale)

    @pl.when(j >= nq)
    def _():
        emit(nk, 1.0)


def _inproj1(x, sc, sh, pos, freq, slo, shi, w, n_q_cols, n_k_cols):
    L, d = x.shape
    n = w.shape[1]
    tm, tn = 512, 512
    assert n_q_cols % tn == 0 and n - n_q_cols == tn
    kern = functools.partial(_inproj1_kernel, nq=n_q_cols // tn, nk=n_k_cols // LANES,
                             scale=1.0 / math.sqrt(SWA_HEAD_DIM))
    fix = lambda i, j: (0, 0)
    return pl.pallas_call(
        kern,
        grid=(L // tm, n // tn),
        in_specs=[pl.BlockSpec((tm, d), lambda i, j: (i, 0)),
                  pl.BlockSpec((1, d), fix), pl.BlockSpec((1, d), fix),
                  pl.BlockSpec((tm, 1), lambda i, j: (i, 0)),
                  pl.BlockSpec((1, LANES), fix), pl.BlockSpec((1, LANES), fix),
                  pl.BlockSpec((1, LANES), fix),
                  pl.BlockSpec((d, tn), lambda i, j: (0, j))],
        out_specs=pl.BlockSpec((tm, tn), lambda i, j: (i, j)),
        out_shape=jax.ShapeDtypeStruct((L, n), BF16),
        scratch_shapes=[pltpu.VMEM((tm, d), BF16), pltpu.VMEM((tm, LANES), F32),
                        pltpu.VMEM((tm, LANES), F32), pltpu.VMEM((tm, LANES), F32)],
        compiler_params=_cparams(2, 40),
        name="inproj1",
    )(x, sc, sh, pos, freq, slo, shi, w)


def _swa_kernel(sink_ref, q_ref, kc_ref, kp_ref, vc_ref, vp_ref, o_ref, *, n_kv):
    i = pl.program_id(0)
    W = WINDOW
    dh = SWA_HEAD_DIM
    qi = lax.broadcasted_iota(jnp.int32, (W, W), 0)
    kj = lax.broadcasted_iota(jnp.int32, (W, W), 1)
    mask_prev = jnp.logical_and(kj > qi, i > 0)
    mask_cur = kj <= qi
    for kvh in range(n_kv):
        kc = kc_ref[:, kvh * dh:(kvh + 1) * dh]
        kp = kp_ref[:, kvh * dh:(kvh + 1) * dh]
        vc = vc_ref[:, kvh * dh:(kvh + 1) * dh]
        vp = vp_ref[:, kvh * dh:(kvh + 1) * dh]
        for gp in range(SWA_Q_PER_KV // 2):
            outs = []
            for g in (2 * gp, 2 * gp + 1):
                hq = kvh * SWA_Q_PER_KV + g
                q = q_ref[:, hq * dh:(hq + 1) * dh]
                sp = jnp.where(mask_prev, _dot_nt(q, kp), NEG_BIG)
                sc = jnp.where(mask_cur, _dot_nt(q, kc), NEG_BIG)
                sink = sink_ref[hq]
                m = jnp.maximum(jnp.maximum(jnp.max(sp, axis=1, keepdims=True),
                                            jnp.max(sc, axis=1, keepdims=True)), sink)
                pp = jnp.exp(sp - m)
                pc = jnp.exp(sc - m)
                den = (jnp.sum(pp, axis=1, keepdims=True)
                       + jnp.sum(pc, axis=1, keepdims=True) + jnp.exp(sink - m))
                o = _dot(pp.astype(BF16), vp) + _dot(pc.astype(BF16), vc)
                outs.append(o / den)
            lo = 2 * gp * dh + kvh * SWA_Q_PER_KV * dh
            o_ref[:, lo:lo + 2 * dh] = jnp.concatenate(outs, axis=1).astype(BF16)


def _swa(qkv, sinks, n_q_cols, n_kv):
    L = qkv.shape[0]
    W = WINDOW
    kw = n_kv * SWA_HEAD_DIM
    kb = n_q_cols // kw
    prev = lambda i, s: (jnp.maximum(i - 1, 0), kb)
    prev_v = lambda i, s: (jnp.maximum(i - 1, 0), kb + 1)
    grid_spec = pltpu.PrefetchScalarGridSpec(
        num_scalar_prefetch=1,
        grid=(L // W,),
        in_specs=[pl.BlockSpec((W, n_q_cols), lambda i, s: (i, 0)),
                  pl.BlockSpec((W, kw), lambda i, s: (i, kb)),
                  pl.BlockSpec((W, kw), prev),
                  pl.BlockSpec((W, kw), lambda i, s: (i, kb + 1)),
                  pl.BlockSpec((W, kw), prev_v)],
        out_specs=pl.BlockSpec((W, n_q_cols), lambda i, s: (i, 0)),
    )
    return pl.pallas_call(
        functools.partial(_swa_kernel, n_kv=n_kv),
        grid_spec=grid_spec,
        out_shape=jax.ShapeDtypeStruct((L, n_q_cols), BF16),
        compiler_params=_cparams(1, 32),
        name="swa",
    )(sinks, qkv, qkv, qkv, qkv, qkv)


def kernel(x, c, positions, w_in_ab, b_forget, ssm_lambda_re, ssm_lambda_im, ssm_log_dt, ssm_b_re, ssm_b_im, ssm_c_re, ssm_c_im, ssm_d, w_glu, b_glu, w_out_ab, w_in_c, attn_sinks, w_out_c, w_ada, b_ada, ln_mix_g, ln_mix_b, ln_ffn_g, ln_ffn_b, w_ffn_gate, w_ffn_up, w_ffn_down):
    B, L, D = x.shape
    assert B == 1
    x2 = x.reshape(L, D)
    mod = _ada(c, w_ada, b_ada)

    def mods(layer):
        return [mod[layer, :, k * D:(k + 1) * D] for k in range(6)]

    def vec(a):
        return a.reshape(1, -1)

    sh1, sc1, g1, sh2, sc2, g2 = mods(0)
    W = FOX_WIDTH
    w_in = w_in_ab[0]
    w_qku = jnp.concatenate([w_in[:, :2 * W], w_in[:, 3 * W + FOX_HEADS:]], 1).astype(BF16)
    w_vt = w_in[:, 2 * W:3 * W].T.astype(BF16)
    w_f = jnp.pad(w_in[:, 3 * W:3 * W + FOX_HEADS], ((0, 0), (0, LANES - FOX_HEADS))).astype(BF16)
    b_f = jnp.pad(b_forget[0], (0, LANES - FOX_HEADS)).reshape(1, LANES)
    tq = 512
    qa, ka, u, vt = _inproj0(x2, sc1, sh1, w_qku, w_vt, w_f, b_f, tq)
    a_out = _fox(qa, ka, vt, tq)

    T = SSM_CHUNK
    nrow = L // T
    nsub = nrow // SUBLANES
    width = u.shape[1]
    u_s = u.reshape(SUBLANES, nsub, T, width).transpose(2, 1, 0, 3).reshape(T, nrow, width)
    tables = _ssm_tables(ssm_lambda_re[0], ssm_lambda_im[0], ssm_log_dt[0], ssm_b_re[0],
                         ssm_b_im[0], ssm_c_re[0], ssm_c_im[0], ssm_d[0], nsub)
    y_s = _ssm(u_s, *tables)
    y = y_s.reshape(T, nsub, SUBLANES, width).transpose(2, 1, 0, 3).reshape(L, width)

    x2 = _mixout(a_out, y, x2, w_glu[0].astype(BF16), vec(b_glu[0]), w_out_ab[0].astype(BF16),
                 g1, vec(ln_mix_g[0]), vec(ln_mix_b[0]))
    x2 = _ffn(x2, sc2, sh2, w_ffn_gate[0].astype(BF16), w_ffn_up[0].astype(BF16),
              w_ffn_down[0].astype(BF16), g2, vec(ln_ffn_g[0]), vec(ln_ffn_b[0]))

    sh1, sc1, g1, sh2, sc2, g2 = mods(1)
    n_q = w_out_c.shape[1]
    n_kv = (w_in_c.shape[2] - n_q) // (2 * SWA_HEAD_DIM)
    half = ROT_DIM // 2
    inv_freq = jnp.power(jnp.float32(ROPE_THETA), -jnp.arange(half, dtype=F32) * (2.0 / ROT_DIM))
    dlane = jnp.arange(LANES) % SWA_HEAD_DIM
    freq = jnp.where(dlane < ROT_DIM, inv_freq[dlane % half], 0.0).reshape(1, LANES)
    slo = jnp.where(dlane < half, -1.0, 0.0).astype(F32).reshape(1, LANES)
    shi = jnp.where((dlane >= half) & (dlane < ROT_DIM), 1.0, 0.0).astype(F32).reshape(1, LANES)
    pos = positions.reshape(L, 1).astype(F32)
    qkv = _inproj1(x2, sc1, sh1, pos, freq, slo, shi, w_in_c[0].astype(BF16),
                   n_q, n_kv * SWA_HEAD_DIM)
    a1 = _swa(qkv, attn_sinks[0], n_q, n_kv)
    x2 = _mixout(a1, None, x2, None, None, w_out_c[0].astype(BF16), g1,
                 vec(ln_mix_g[1]), vec(ln_mix_b[1]))
    x2 = _ffn(x2, sc2, sh2, w_ffn_gate[1].astype(BF16), w_ffn_up[1].astype(BF16),
              w_ffn_down[1].astype(BF16), g2, vec(ln_ffn_g[1]), vec(ln_ffn_b[1]))
    return x2.reshape(B, L, D)
```

```python
import functools
import math

import jax
import jax.numpy as jnp
from jax import lax
from jax.experimental import pallas as pl
from jax.experimental.pallas import tpu as pltpu

F32 = jnp.float32
BF16 = jnp.bfloat16

FOX_HEADS = 8
FOX_HEAD_DIM = 128
FOX_WIDTH = FOX_HEADS * FOX_HEAD_DIM
SSM_GROUP = 16
SSM_STATE = 64
SWA_HEAD_DIM = 64
SWA_Q_PER_KV = 8
WINDOW = 128
ROT_DIM = SWA_HEAD_DIM // 4
ROPE_THETA = 500000.0
DEPTH = 2
DEEPNORM_ALPHA = (2 * DEPTH) ** 0.25
LN_EPS = 1e-5

LANES = 128
SUBLANES = 8
NEG_BIG = -1e30

SSM_CHUNK = 16
SSM_BLOCK_GROUPS = LANES // SSM_GROUP


def _cparams(n_axes, vmem_mb):
    return pltpu.CompilerParams(
        dimension_semantics=("arbitrary",) * n_axes,
        vmem_limit_bytes=vmem_mb * 1024 * 1024)


def _sigmoid(x):
    return 1.0 / (1.0 + jnp.exp(-x))


def _layer_norm(r, g, b):
    mu = jnp.mean(r, axis=-1, keepdims=True)
    d = r - mu
    var = jnp.mean(d * d, axis=-1, keepdims=True)
    return d * lax.rsqrt(var + LN_EPS) * g + b


def _dot(a, b):
    return jnp.dot(a, b, preferred_element_type=F32)


def _dot_nt(a, b):
    return lax.dot_general(a, b, (((1,), (1,)), ((), ())), preferred_element_type=F32)


def _ada_kernel(c_ref, w_ref, b_ref, o_ref):
    c = c_ref[...]
    sc = c * _sigmoid(c)
    o_ref[0] = jnp.dot(sc, w_ref[0], precision=lax.Precision.HIGHEST,
                       preferred_element_type=F32) + b_ref[0]


def _ada(c, w_ada, b_ada):
    depth, d, n = w_ada.shape
    tn = 1536
    c8 = jnp.broadcast_to(c, (SUBLANES, d))
    out = pl.pallas_call(
        _ada_kernel,
        grid=(depth, n // tn),
        in_specs=[pl.BlockSpec((SUBLANES, d), lambda l, j: (0, 0)),
                  pl.BlockSpec((1, d, tn), lambda l, j: (l, 0, j)),
                  pl.BlockSpec((1, 1, tn), lambda l, j: (l, 0, j))],
        out_specs=pl.BlockSpec((1, SUBLANES, tn), lambda l, j: (l, 0, j)),
        out_shape=jax.ShapeDtypeStruct((depth, SUBLANES, n), F32),
        compiler_params=_cparams(2, 40),
        name="ada",
    )(c8, w_ada, b_ada.reshape(depth, 1, n))
    return out[:, 0:1, :]


N_SPLIT = 3
ONES_ROWS = 16
LOG2E = 1.0 / math.log(2.0)


def _inproj0_kernel(x_ref, sc_ref, sh_ref, w_ref, wvt_ref, wf_ref, bf_ref, pq_ref, pk_ref,
                    cq_ref, ck_ref, qa_ref, ka_ref, u_ref, vt_ref, h_sc, carry_sc,
                    *, tm, scale):
    i = pl.program_id(0)
    j = pl.program_id(1)
    hd = FOX_HEAD_DIM

    def put_heads(dst_ref, val, off):
        for h in range(FOX_HEADS):
            dst_ref[:, 2 * h * hd + off:2 * h * hd + off + hd] = (
                val[:, h * hd:(h + 1) * hd].astype(BF16))

    @pl.when(j == 0)
    def _():
        hb = (x_ref[...] * (1.0 + sc_ref[...]) + sh_ref[...]).astype(BF16)
        h_sc[...] = hb
        z = _dot(hb, wf_ref[...]) + bf_ref[...]
        lf = jnp.minimum(z, 0.0) - jnp.log1p(jnp.exp(-jnp.abs(z)))

        def split(v):
            p1 = v.astype(BF16)
            r1 = v - p1.astype(F32)
            p2 = r1.astype(BF16)
            return p1, p2, (r1 - p2.astype(F32)).astype(BF16)

        row = lax.broadcasted_iota(jnp.int32, (tm, tm), 0)
        col = lax.broadcasted_iota(jnp.int32, (tm, tm), 1)
        tri = (col <= row).astype(BF16)
        cum = sum(_dot(tri, p) for p in split(lf))

        @pl.when(i == 0)
        def _():
            carry_sc[...] = jnp.zeros_like(carry_sc)

        f = cum + carry_sc[...]
        carry_sc[...] = f[tm - 1:tm, :]
        fp = jnp.concatenate(split(f * LOG2E), axis=1)
        put_heads(qa_ref, _dot(fp, pq_ref[...]) + cq_ref[...], hd)
        put_heads(ka_ref, _dot(fp, pk_ref[...]) + ck_ref[...], hd)
        put_heads(qa_ref, _dot(hb, w_ref[...]) * scale, 0)

    @pl.when(j == 1)
    def _():
        put_heads(ka_ref, _dot(h_sc[...], w_ref[...]), 0)

    @pl.when(j == 2)
    def _():
        u_ref[...] = _dot(h_sc[...], w_ref[...]).astype(BF16)

    @pl.when(j == 3)
    def _():
        vt = _dot_nt(wvt_ref[...], h_sc[...])
        vt_ref[:, :, 0:hd, :] = vt.astype(BF16).reshape(FOX_HEADS, 1, hd, tm)
        vt_ref[:, :, hd:, :] = jnp.ones((FOX_HEADS, 1, ONES_ROWS, tm), BF16)


def _inproj0(x, sc, sh, w_qku, w_vt, wf, bf, tm):
    L, d = x.shape
    W = FOX_WIDTH
    hd = FOX_HEAD_DIM
    piece = jnp.arange(N_SPLIT)
    head = jnp.arange(FOX_HEADS)
    rows = (piece[:, None] * LANES + head[None, :]).reshape(-1)
    cols_k = (head[None, :] * hd + piece[:, None]).reshape(-1)
    cols_q = cols_k + N_SPLIT
    pq = jnp.zeros((N_SPLIT * LANES, W), F32).at[rows, cols_q].set(1.0).astype(BF16)
    pk = jnp.zeros((N_SPLIT * LANES, W), F32).at[rows, cols_k].set(-1.0).astype(BF16)
    lane = jnp.arange(W) % hd
    cq = (lane < N_SPLIT).astype(F32).reshape(1, W)
    ck = ((lane >= N_SPLIT) & (lane < 2 * N_SPLIT)).astype(F32).reshape(1, W)
    kern = functools.partial(_inproj0_kernel, tm=tm, scale=LOG2E / math.sqrt(hd))
    hv = hd + ONES_ROWS
    fix = lambda i, j: (0, 0)
    row = lambda i, j: (i, 0)
    return pl.pallas_call(
        kern,
        grid=(L // tm, 4),
        in_specs=[pl.BlockSpec((tm, d), row),
                  pl.BlockSpec((1, d), fix),
                  pl.BlockSpec((1, d), fix),
                  pl.BlockSpec((d, W), lambda i, j: (0, jnp.minimum(j, 2))),
                  pl.BlockSpec((W, d), fix),
                  pl.BlockSpec((d, LANES), fix),
                  pl.BlockSpec((1, LANES), fix),
                  pl.BlockSpec((N_SPLIT * LANES, W), fix),
                  pl.BlockSpec((N_SPLIT * LANES, W), fix),
                  pl.BlockSpec((1, W), fix),
                  pl.BlockSpec((1, W), fix)],
        out_specs=[pl.BlockSpec((tm, 2 * W), row),
                   pl.BlockSpec((tm, 2 * W), row),
                   pl.BlockSpec((tm, W), row),
                   pl.BlockSpec((FOX_HEADS, 1, hv, tm), lambda i, j: (0, i, 0, 0))],
        out_shape=[jax.ShapeDtypeStruct((L, 2 * W), BF16),
                   jax.ShapeDtypeStruct((L, 2 * W), BF16),
                   jax.ShapeDtypeStruct((L, W), BF16),
                   jax.ShapeDtypeStruct((FOX_HEADS, L // tm, hv, tm), BF16)],
        scratch_shapes=[pltpu.VMEM((tm, d), BF16), pltpu.VMEM((1, LANES), F32)],
        compiler_params=_cparams(2, 48),
        name="inproj0",
    )(x, sc, sh, w_qku, w_vt, wf, bf, pq, pk, cq, ck)


M_INIT = -1e29


def _fox_kernel(qa_ref, ka_ref, vt_ref, o_ref, s0, s1, p0, p1, al0, al1, mb0, mb1, m_sc,
                acc_sc, mask_sc, *, tq):
    first = jnp.logical_and(pl.program_id(0) == 0, pl.program_id(1) == 0)
    i = pl.program_id(1)

    @pl.when(first)
    def _():
        key = lax.broadcasted_iota(jnp.int32, (tq, tq), 0)
        qry = lax.broadcasted_iota(jnp.int32, (tq, tq), 1)
        mask_sc[0] = jnp.zeros((tq, tq), F32)
        mask_sc[1] = jnp.where(key <= qry, 0.0, NEG_BIG)
        mask_sc[2] = jnp.full((tq, tq), NEG_BIG, F32)

    q = qa_ref[...]
    hd = o_ref.shape[1]
    s1[...] = jnp.full_like(s1, NEG_BIG)
    mb1[...] = jnp.full_like(mb1, NEG_BIG)
    p0[...] = jnp.zeros_like(p0)
    al0[...] = jnp.ones_like(al0)
    m_sc[...] = jnp.full_like(m_sc, M_INIT)
    acc_sc[...] = jnp.zeros_like(acc_sc)

    def scores(t, s_ref, mb_ref, masked):
        start = pl.multiple_of(jnp.minimum(t, i) * tq, tq)
        s = _dot_nt(ka_ref[pl.ds(start, tq), :], q)
        if masked:
            s = s + mask_sc[jnp.clip(t - i + 1, 0, 2)]
        s_ref[...] = s
        mb_ref[...] = jnp.max(s, axis=0, keepdims=True)

    def softmax(s_ref, mb_ref, p_ref, al_ref):
        m_prev = m_sc[...]
        m_new = jnp.maximum(m_prev, mb_ref[...])
        m_sc[...] = m_new
        al_ref[...] = jnp.exp2(m_prev - m_new)
        p_ref[...] = jnp.exp2(s_ref[...] - m_new).astype(BF16)

    def values(t, p_ref, al_ref):
        blk = jnp.clip(t, 0, i)
        acc_sc[...] = al_ref[...] * acc_sc[...] + _dot(vt_ref[0, blk], p_ref[...])

    def pair(u, carry, masked):
        t = 2 * u
        scores(t, s0, mb0, masked)
        softmax(s1, mb1, p1, al1)
        values(t - 2, p0, al0)
        scores(t + 1, s1, mb1, masked)
        softmax(s0, mb0, p0, al0)
        values(t - 1, p1, al1)
        return carry

    n_plain = i // 2
    lax.fori_loop(0, n_plain, functools.partial(pair, masked=False), 0)
    lax.fori_loop(n_plain, (i + 4) // 2, functools.partial(pair, masked=True), 0)
    acc = acc_sc[...]
    o_ref[...] = (acc[0:hd] / acc[hd:hd + 1]).T.astype(BF16)


def _fox(qa, ka, vt, tq):
    L = qa.shape[0]
    hd = FOX_HEAD_DIM
    hv = hd + ONES_ROWS
    assert vt.shape == (FOX_HEADS, L // tq, hv, tq)
    row = pltpu.VMEM((1, tq), F32)
    return pl.pallas_call(
        functools.partial(_fox_kernel, tq=tq),
        grid=(FOX_HEADS, L // tq),
        in_specs=[pl.BlockSpec((tq, 2 * hd), lambda h, i: (i, h)),
                  pl.BlockSpec((L, 2 * hd), lambda h, i: (0, h)),
                  pl.BlockSpec((1, L // tq, hv, tq), lambda h, i: (h, 0, 0, 0))],
        out_specs=pl.BlockSpec((tq, hd), lambda h, i: (i, h)),
        out_shape=jax.ShapeDtypeStruct((L, FOX_WIDTH), BF16),
        scratch_shapes=[pltpu.VMEM((tq, tq), F32), pltpu.VMEM((tq, tq), F32),
                        pltpu.VMEM((tq, tq), BF16), pltpu.VMEM((tq, tq), BF16),
                        row, row, row, row, row,
                        pltpu.VMEM((hv, tq), F32), pltpu.VMEM((3, tq, tq), F32)],
        compiler_params=_cparams(2, 48),
        name="fox",
    )(qa, ka, vt)


def _taps_kernel(cre_ref, cim_ref, pre_ref, pim_ref, bre_ref, bim_ref, o_ref):
    T, H, P = SSM_CHUNK, SSM_GROUP, SSM_STATE
    nt = (((1,), (1,)), ((), ()))
    for g in range(SSM_BLOCK_GROUPS):
        cre = cre_ref[g][None]
        cim = cim_ref[g][None]
        pre = pre_ref[g][:, None, :]
        pim = pim_ref[g][:, None, :]
        wre = (cre * pre - cim * pim).reshape(T * H, P)
        wim = (cre * pim + cim * pre).reshape(T * H, P)
        k = (lax.dot_general(bre_ref[g], wre, nt, precision=lax.Precision.HIGHEST,
                             preferred_element_type=F32)
             - lax.dot_general(bim_ref[g], wim, nt, precision=lax.Precision.HIGHEST,
                               preferred_element_type=F32))
        o_ref[g] = k


def _taps(c_re, c_im, pw_re, pw_im, bbt_re, bbt_im):
    G = c_re.shape[0]
    gb = SSM_BLOCK_GROUPS
    T, H, P = SSM_CHUNK, SSM_GROUP, SSM_STATE
    return pl.pallas_call(
        _taps_kernel,
        grid=(G // gb,),
        in_specs=[pl.BlockSpec((gb, H, P), lambda b: (b, 0, 0)),
                  pl.BlockSpec((gb, H, P), lambda b: (b, 0, 0)),
                  pl.BlockSpec((gb, T, P), lambda b: (b, 0, 0)),
                  pl.BlockSpec((gb, T, P), lambda b: (b, 0, 0)),
                  pl.BlockSpec((gb, H, P), lambda b: (b, 0, 0)),
                  pl.BlockSpec((gb, H, P), lambda b: (b, 0, 0))],
        out_specs=pl.BlockSpec((gb, H, T * H), lambda b: (b, 0, 0)),
        out_shape=jax.ShapeDtypeStruct((G, H, T * H), F32),
        compiler_params=_cparams(1, 32),
        name="taps",
    )(c_re, c_im, pw_re, pw_im, bbt_re, bbt_im)


def _ssm_kernel(u_ref, base_ref, w2r_ref, w2i_ref, v3r_ref, v3i_ref, rs_ref, r3_ref,
                ar_ref, ai_ref, pr_ref, pi_ref, a2r_ref, a2i_ref, o_ref,
                ucat_sc, s_sc, x_sc, m1_sc, m2_sc, m3_sc, *, nrow):
    tp = pl.program_id(1)
    T = SSM_CHUNK
    half = x_sc.shape[1] // 2
    ngrp = nrow // SUBLANES
    lg_h = SSM_GROUP.bit_length() - 1
    lg_p = SSM_STATE.bit_length() - 1

    def same_group(shape, row_shift, col_and, col_shift):
        r = lax.shift_right_logical(lax.broadcasted_iota(jnp.int32, shape, 0), row_shift)
        c = lax.broadcasted_iota(jnp.int32, shape, 1)
        c = lax.shift_right_logical(jnp.bitwise_and(c, col_and), col_shift)
        return (r == c).astype(F32)

    @pl.when(tp == 0)
    def _():
        for s in range(T):
            ucat_sc[:, s * LANES:(s + 1) * LANES] = u_ref[s]
        mask3 = same_group((LANES, half), lg_h, half - 1, lg_p)
        for v_ref, lo in ((v3r_ref, 0), (v3i_ref, half)):
            e = _dot(v_ref[0].astype(BF16), r3_ref[...])
            for s in range(T):
                m3_sc[s * LANES:(s + 1) * LANES, lo:lo + half] = (
                    e[s * LANES:(s + 1) * LANES] * mask3).astype(BF16)
        s_sc[...] = _dot(ucat_sc[...], m3_sc[...])
        ar = ar_ref[0]
        ai = ai_ref[0]

        def step(r, carry):
            xr, xi = carry
            row = pl.multiple_of(r * SUBLANES, SUBLANES)
            x_sc[pl.ds(row, SUBLANES), 0:half] = xr
            x_sc[pl.ds(row, SUBLANES), half:2 * half] = xi
            sr = s_sc[pl.ds(row, SUBLANES), 0:half]
            si = s_sc[pl.ds(row, SUBLANES), half:2 * half]
            return ar * xr - ai * xi + sr, ar * xi + ai * xr + si

        zero = jnp.zeros((SUBLANES, half), F32)
        er, ei = lax.fori_loop(0, ngrp, step, (zero, zero))

        a2r = a2r_ref[0]
        a2i = a2i_ref[0]
        sub = lax.broadcasted_iota(jnp.int32, (SUBLANES, half), 0)
        zr = jnp.zeros((1, half), F32)
        zi = jnp.zeros((1, half), F32)
        zr_all = zero
        zi_all = zero
        for jj in range(1, SUBLANES):
            nzr = a2r * zr - a2i * zi + er[jj - 1:jj]
            nzi = a2r * zi + a2i * zr + ei[jj - 1:jj]
            zr, zi = nzr, nzi
            zr_all = jnp.where(sub == jj, zr, zr_all)
            zi_all = jnp.where(sub == jj, zi, zi_all)

        def fix(r, carry):
            row = pl.multiple_of(r * SUBLANES, SUBLANES)
            pr = pr_ref[0, pl.ds(r, 1), :]
            pi = pi_ref[0, pl.ds(r, 1), :]
            x_sc[pl.ds(row, SUBLANES), 0:half] = (
                x_sc[pl.ds(row, SUBLANES), 0:half] + pr * zr_all - pi * zi_all)
            x_sc[pl.ds(row, SUBLANES), half:2 * half] = (
                x_sc[pl.ds(row, SUBLANES), half:2 * half] + pr * zi_all + pi * zr_all)
            return carry

        lax.fori_loop(0, ngrp, fix, 0)

    base = base_ref[0].astype(BF16)
    mask1 = same_group((LANES, 2 * LANES), lg_h, LANES - 1, lg_h)
    for s in range(T):
        m1_sc[s * LANES:(s + 1) * LANES, :] = (_dot(base, rs_ref[s]) * mask1).astype(BF16)
    mask2 = same_group((half, 2 * LANES), lg_p, LANES - 1, lg_h)
    m2_sc[0:half, :] = (_dot(w2r_ref[0].astype(BF16), rs_ref[0]) * mask2).astype(BF16)
    m2_sc[half:2 * half, :] = (-_dot(w2i_ref[0].astype(BF16), rs_ref[0]) * mask2).astype(BF16)

    y = _dot(ucat_sc[...], m1_sc[...]) + _dot(x_sc[...].astype(BF16), m2_sc[...])
    o_ref[0] = y[:, 0:LANES].astype(BF16)
    o_ref[1] = y[:, LANES:2 * LANES].astype(BF16)


def _ssm(u_s, base, w2_re, w2_im, v3_re, v3_im, rs, r3, a_re, a_im, p_re, p_im, a2_re, a2_im):
    T, nrow, width = u_s.shape
    nb = width // LANES
    half = r3.shape[1]
    nstate = 2 * half
    ngrp = nrow // SUBLANES
    th = rs.shape[1]
    P = r3.shape[0]
    kern = functools.partial(_ssm_kernel, nrow=nrow)
    return pl.pallas_call(
        kern,
        grid=(nb, T // 2),
        in_specs=[pl.BlockSpec((T, nrow, LANES), lambda b, t: (0, 0, b)),
                  pl.BlockSpec((1, LANES, th), lambda b, t: (b, 0, 0)),
                  pl.BlockSpec((1, half, th), lambda b, t: (b, 0, 0)),
                  pl.BlockSpec((1, half, th), lambda b, t: (b, 0, 0)),
                  pl.BlockSpec((1, T * LANES, P), lambda b, t: (b, 0, 0)),
                  pl.BlockSpec((1, T * LANES, P), lambda b, t: (b, 0, 0)),
                  pl.BlockSpec((T, th, 2 * LANES), lambda b, t: (0, 0, t)),
                  pl.BlockSpec((P, half), lambda b, t: (0, 0)),
                  pl.BlockSpec((1, 1, half), lambda b, t: (b, 0, 0)),
                  pl.BlockSpec((1, 1, half), lambda b, t: (b, 0, 0)),
                  pl.BlockSpec((1, ngrp, half), lambda b, t: (b, 0, 0)),
                  pl.BlockSpec((1, ngrp, half), lambda b, t: (b, 0, 0)),
                  pl.BlockSpec((1, 1, half), lambda b, t: (b, 0, 0)),
                  pl.BlockSpec((1, 1, half), lambda b, t: (b, 0, 0))],
        out_specs=pl.BlockSpec((2, nrow, LANES), lambda b, t: (t, 0, b)),
        out_shape=jax.ShapeDtypeStruct((T, nrow, width), BF16),
        scratch_shapes=[pltpu.VMEM((nrow, T * LANES), BF16),
                        pltpu.VMEM((nrow, nstate), F32),
                        pltpu.VMEM((nrow, nstate), F32),
                        pltpu.VMEM((T * LANES, 2 * LANES), BF16),
                        pltpu.VMEM((nstate, 2 * LANES), BF16),
                        pltpu.VMEM((T * LANES, nstate), BF16)],
        compiler_params=_cparams(2, 48),
        name="ssm",
    )(u_s, base, w2_re, w2_im, v3_re, v3_im, rs, r3, a_re, a_im, p_re, p_im, a2_re, a2_im)


def _ssm_tables(lam_re, lam_im, log_dt, b_re, b_im, c_re, c_im, d_skip, ngrp):
    G, P = lam_re.shape
    H = SSM_GROUP
    T = SSM_CHUNK
    gb = SSM_BLOCK_GROUPS
    nb = G // gb
    dt = jnp.exp(log_dt)[:, None]
    mag = jnp.exp(lam_re * dt)
    lb_re = mag * jnp.cos(lam_im * dt)
    lb_im = mag * jnp.sin(lam_im * dt)
    den = lam_re * lam_re + lam_im * lam_im
    nr = lb_re - 1.0
    q_re = (nr * lam_re + lb_im * lam_im) / den
    q_im = (lb_im * lam_re - nr * lam_im) / den
    bb_re = q_re[..., None] * b_re - q_im[..., None] * b_im
    bb_im = q_re[..., None] * b_im + q_im[..., None] * b_re

    def cmul(ar, ai, br, bi):
        return ar * br - ai * bi, ar * bi + ai * br

    def powers(ar, ai, n):
        pr, pi = jnp.ones_like(ar)[None], jnp.zeros_like(ai)[None]
        sr, si = ar, ai
        while pr.shape[0] < n:
            nr_, ni_ = cmul(pr, pi, sr[None], si[None])
            pr = jnp.concatenate([pr, nr_], 0)
            pi = jnp.concatenate([pi, ni_], 0)
            sr, si = cmul(sr, si, sr, si)
        return pr[:n], pi[:n], sr, si

    pw_re, pw_im, _, _ = powers(lb_re, lb_im, 2 * T)
    aT_re, aT_im = pw_re[T], pw_im[T]
    pT_re, pT_im, a2_re, a2_im = powers(aT_re, aT_im, ngrp)

    bbt_re = jnp.transpose(bb_re, (0, 2, 1))
    bbt_im = jnp.transpose(bb_im, (0, 2, 1))
    taps = _taps(c_re, c_im,
                 jnp.transpose(pw_re[:T], (1, 0, 2)), jnp.transpose(pw_im[:T], (1, 0, 2)),
                 bbt_re, bbt_im)
    jj = jnp.arange(H)
    taps = taps.at[:, jj, jj].add(d_skip)
    base = taps.reshape(nb, gb * H, T * H)

    er = pw_re[:T][::-1][:, :, None, :]
    ei = pw_im[:T][::-1][:, :, None, :]
    v_re = er * bbt_re[None] - ei * bbt_im[None]
    v_im = er * bbt_im[None] + ei * bbt_re[None]

    def rows3(v):
        v = v.reshape(T, nb, gb * H, P)
        return jnp.transpose(v, (1, 0, 2, 3)).reshape(nb, T * LANES, P)

    fr = jnp.transpose(pw_re[1:T + 1], (1, 2, 0))[..., None]
    fi = jnp.transpose(pw_im[1:T + 1], (1, 2, 0))[..., None]
    ct_re = jnp.transpose(c_re, (0, 2, 1))[:, :, None, :]
    ct_im = jnp.transpose(c_im, (0, 2, 1))[:, :, None, :]
    w2_re = (ct_re * fr - ct_im * fi).reshape(nb, gb * P, T * H)
    w2_im = (ct_re * fi + ct_im * fr).reshape(nb, gb * P, T * H)

    rr = jnp.arange(T * H)
    cc = jnp.arange(T * LANES)
    same_i = (rr % H)[None, :, None] == (cc % H)[None, None, :]
    shifted = ((rr // H)[None, :, None] + jnp.arange(T)[:, None, None]
               == (cc // LANES)[None, None, :])
    rs = (same_i & shifted).astype(BF16)
    r3 = (jnp.arange(P)[:, None] == (jnp.arange(gb * P) % P)[None, :]).astype(BF16)

    def lanes(a):
        lead = a.shape[:-2]
        a = a.reshape(lead + (nb, gb * P))
        return jnp.moveaxis(a, -2, 0)

    a_re = lanes(aT_re)[:, None, :]
    a_im = lanes(aT_im)[:, None, :]
    p_re = lanes(pT_re)
    p_im = lanes(pT_im)
    a2_re = lanes(a2_re)[:, None, :]
    a2_im = lanes(a2_im)[:, None, :]
    return (base, w2_re, w2_im, rows3(v_re), rows3(v_im), rs, r3,
            a_re, a_im, p_re, p_im, a2_re, a2_im)


def _mixout_kernel(*refs, glu):
    if glu:
        (a_ref, y_ref, x_ref, wglu_ref, bglu_ref, wo_ref, g_ref, lng_ref, lnb_ref,
         o_ref) = refs
        y = y_ref[...].astype(F32)
        s = y * (0.5 * (1.0 + jnp.tanh(math.sqrt(2.0 / math.pi)
                                       * (y + 0.044715 * (y * y * y)))))
        z = _dot(s.astype(BF16), wglu_ref[...]) + bglu_ref[...]
        s2 = (s * _sigmoid(z)).astype(BF16)
        wa = a_ref.shape[1]
        mix = _dot(a_ref[...], wo_ref[0:wa, :]) + _dot(s2, wo_ref[wa:, :])
    else:
        a_ref, x_ref, wo_ref, g_ref, lng_ref, lnb_ref, o_ref = refs
        mix = _dot(a_ref[...], wo_ref[...])
    r = DEEPNORM_ALPHA * x_ref[...] + g_ref[...] * mix
    o_ref[...] = _layer_norm(r, lng_ref[...], lnb_ref[...])


def _mixout(a, y, x, wglu, bglu, wo, g, lng, lnb):
    L, d = x.shape
    tm = 512
    glu = y is not None
    row = lambda i: (i, 0)
    fix = lambda i: (0, 0)
    vec = pl.BlockSpec((1, d), fix)
    if glu:
        ws = y.shape[1]
        args = (a, y, x, wglu, bglu, wo, g, lng, lnb)
        in_specs = [pl.BlockSpec((tm, a.shape[1]), row), pl.BlockSpec((tm, ws), row),
                    pl.BlockSpec((tm, d), row), pl.BlockSpec((ws, ws), fix),
                    pl.BlockSpec((1, ws), fix), pl.BlockSpec(wo.shape, fix), vec, vec, vec]
    else:
        args = (a, x, wo, g, lng, lnb)
        in_specs = [pl.BlockSpec((tm, a.shape[1]), row), pl.BlockSpec((tm, d), row),
                    pl.BlockSpec(wo.shape, fix), vec, vec, vec]
    return pl.pallas_call(
        functools.partial(_mixout_kernel, glu=glu),
        grid=(L // tm,),
        in_specs=in_specs,
        out_specs=pl.BlockSpec((tm, d), row),
        out_shape=jax.ShapeDtypeStruct((L, d), F32),
        compiler_params=_cparams(1, 48),
        name="mixout_glu" if glu else "mixout",
    )(*args)


def _ffn_kernel(x_ref, sc_ref, sh_ref, wg_ref, wu_ref, wd_ref, g_ref, lng_ref, lnb_ref,
                o_ref, h_sc, acc_sc):
    j = pl.program_id(1)

    @pl.when(j == 0)
    def _():
        h_sc[...] = (x_ref[...] * (1.0 + sc_ref[...]) + sh_ref[...]).astype(BF16)
        acc_sc[...] = jnp.zeros_like(acc_sc)

    h = h_sc[...]
    gt = _dot(h, wg_ref[0])
    ut = _dot(h, wu_ref[0])
    act = (gt * _sigmoid(gt) * ut).astype(BF16)
    acc_sc[...] += _dot(act, wd_ref[0])

    @pl.when(j == pl.num_programs(1) - 1)
    def _():
        r = DEEPNORM_ALPHA * x_ref[...] + g_ref[...] * acc_sc[...]
        o_ref[...] = _layer_norm(r, lng_ref[...], lnb_ref[...])


def _ffn(x, sc, sh, wg, wu, wd, layer, g, lng, lnb):
    L, d = x.shape
    dff = wg.shape[2]
    tm, tf = 512, 512
    row = lambda i, j: (i, 0)
    vec = pl.BlockSpec((1, d), lambda i, j: (0, 0))
    return pl.pallas_call(
        _ffn_kernel,
        grid=(L // tm, dff // tf),
        in_specs=[pl.BlockSpec((tm, d), row), vec, vec,
                  pl.BlockSpec((1, d, tf), lambda i, j: (layer, 0, j)),
                  pl.BlockSpec((1, d, tf), lambda i, j: (layer, 0, j)),
                  pl.BlockSpec((1, tf, d), lambda i, j: (layer, j, 0)),
                  vec, vec, vec],
        out_specs=pl.BlockSpec((tm, d), row),
        out_shape=jax.ShapeDtypeStruct((L, d), F32),
        scratch_shapes=[pltpu.VMEM((tm, d), BF16), pltpu.VMEM((tm, d), F32)],
        compiler_params=_cparams(2, 48),
        name="ffn",
    )(x, sc, sh, wg, wu, wd, g, lng, lnb)


def _inproj1_kernel(x_ref, sc_ref, sh_ref, posc_ref, posr_ref, freq_ref, slo_ref, shi_ref,
                    fcol_ref, wqt_ref, wk_ref, wvt_ref, qt_ref, k_ref, vt_ref, h_sc,
                    *, nq, scale):
    j = pl.program_id(1)
    half = ROT_DIM // 2
    dh = SWA_HEAD_DIM

    @pl.when(j == 0)
    def _():
        h_sc[...] = (x_ref[...] * (1.0 + sc_ref[...]) + sh_ref[...]).astype(BF16)

    @pl.when(j < nq)
    def _():
        r = _dot_nt(wqt_ref[...], h_sc[...])
        ang = fcol_ref[...] * posr_ref[...]
        cs = jnp.cos(ang)
        sn = jnp.sin(ang)
        for h in range(r.shape[0] // dh):
            b = h * dh
            x1 = r[b:b + half]
            x2 = r[b + half:b + ROT_DIM]
            rot = jnp.concatenate([x1 * cs - x2 * sn, x2 * cs + x1 * sn], axis=0)
            qt_ref[b:b + ROT_DIM, :] = (rot * scale).astype(BF16)
            qt_ref[b + ROT_DIM:b + dh, :] = (r[b + ROT_DIM:b + dh] * scale).astype(BF16)

    @pl.when(j == nq)
    def _():
        hb = h_sc[...]
        kk = _dot(hb, wk_ref[...])
        ang = posc_ref[...] * freq_ref[...]
        cs = jnp.cos(ang)
        sn = jnp.sin(ang)
        sin_lo = sn * slo_ref[...]
        sin_hi = sn * shi_ref[...]
        for c in range(kk.shape[1] // LANES):
            xc = kk[:, c * LANES:(c + 1) * LANES]
            k_ref[:, c * LANES:(c + 1) * LANES] = (
                xc * cs + pltpu.roll(xc, LANES - half, 1) * sin_lo
                + pltpu.roll(xc, half, 1) * sin_hi).astype(BF16)
        vt_ref[...] = _dot_nt(wvt_ref[...], hb).astype(BF16)


def _inproj1(x, sc, sh, pos, w_qt, w_k, w_vt):
    L, d = x.shape
    n_q = w_qt.shape[0]
    n_k = w_k.shape[1]
    tm, tn = 512, 512
    nq = n_q // tn
    half = ROT_DIM // 2
    inv_freq = jnp.power(jnp.float32(ROPE_THETA), -jnp.arange(half, dtype=F32) * (2.0 / ROT_DIM))
    dlane = jnp.arange(LANES) % SWA_HEAD_DIM
    freq = jnp.where(dlane < ROT_DIM, inv_freq[dlane % half], 0.0).reshape(1, LANES)
    slo = jnp.where(dlane < half, -1.0, 0.0).astype(F32).reshape(1, LANES)
    shi = jnp.where((dlane >= half) & (dlane < ROT_DIM), 1.0, 0.0).astype(F32).reshape(1, LANES)
    posf = pos.astype(F32)
    kern = functools.partial(_inproj1_kernel, nq=nq, scale=LOG2E / math.sqrt(SWA_HEAD_DIM))
    fix = lambda i, j: (0, 0)
    qrow = lambda i, j: (jnp.minimum(j, nq - 1), 0)
    return pl.pallas_call(
        kern,
        grid=(L // tm, nq + 1),
        in_specs=[pl.BlockSpec((tm, d), lambda i, j: (i, 0)),
                  pl.BlockSpec((1, d), fix), pl.BlockSpec((1, d), fix),
                  pl.BlockSpec((tm, 1), lambda i, j: (i, 0)),
                  pl.BlockSpec((1, tm), lambda i, j: (0, i)),
                  pl.BlockSpec((1, LANES), fix), pl.BlockSpec((1, LANES), fix),
                  pl.BlockSpec((1, LANES), fix), pl.BlockSpec((half, 1), fix),
                  pl.BlockSpec((tn, d), qrow),
                  pl.BlockSpec((d, n_k), fix),
                  pl.BlockSpec((n_k, d), fix)],
        out_specs=[pl.BlockSpec((tn, tm), lambda i, j: (jnp.minimum(j, nq - 1), i)),
                   pl.BlockSpec((tm, n_k), lambda i, j: (i, 0)),
                   pl.BlockSpec((n_k, tm), lambda i, j: (0, i))],
        out_shape=[jax.ShapeDtypeStruct((n_q, L), BF16),
                   jax.ShapeDtypeStruct((L, n_k), BF16),
                   jax.ShapeDtypeStruct((n_k, L), BF16)],
        scratch_shapes=[pltpu.VMEM((tm, d), BF16)],
        compiler_params=_cparams(2, 40),
        name="inproj1",
    )(x, sc, sh, posf.reshape(L, 1), posf.reshape(1, L), freq, slo, shi,
      inv_freq.reshape(half, 1), w_qt, w_k, w_vt)


def _swa_kernel(sink_ref, qt_ref, kc_ref, kp_ref, vtc_ref, vtp_ref, o_ref, *, n_kv):
    n = pl.program_id(0)
    W = WINDOW
    dh = SWA_HEAD_DIM
    G = SWA_Q_PER_KV
    kj = lax.broadcasted_iota(jnp.int32, (2 * W, W), 0)
    qi = lax.broadcasted_iota(jnp.int32, (2 * W, W), 1)
    lo_key = jnp.where(n > 0, qi, W - 1)
    band = jnp.logical_and(kj > lo_key, kj <= qi + W)
    mask1 = jnp.where(band, 0.0, NEG_BIG)
    mask = jnp.concatenate([mask1] * G, axis=1)
    for kvh in range(n_kv):
        ksl = slice(kvh * dh, (kvh + 1) * dh)
        kk = jnp.concatenate([kp_ref[:, ksl], kc_ref[:, ksl]], axis=0)
        vv = jnp.concatenate([vtp_ref[ksl, :], vtc_ref[ksl, :]], axis=1)
        q = jnp.concatenate([qt_ref[(kvh * G + g) * dh:(kvh * G + g + 1) * dh, :]
                             for g in range(G)], axis=1)
        s = _dot(kk, q) + mask
        sink = jnp.concatenate([jnp.full((1, W), sink_ref[kvh * G + g] * LOG2E, F32)
                                for g in range(G)], axis=1)
        m = jnp.maximum(jnp.max(s, axis=0, keepdims=True), sink)
        p = jnp.exp2(s - m)
        den = jnp.sum(p, axis=0, keepdims=True) + jnp.exp2(sink - m)
        o = _dot(vv, p.astype(BF16)) / den
        for gp in range(G // 2):
            two = jnp.concatenate([o[:, (2 * gp) * W:(2 * gp + 1) * W],
                                   o[:, (2 * gp + 1) * W:(2 * gp + 2) * W]], axis=0)
            lo = (kvh * G + 2 * gp) * dh
            o_ref[:, lo:lo + 2 * dh] = two.T.astype(BF16)


def _swa(qt, k, vt, sinks):
    n_q, L = qt.shape
    kw = k.shape[1]
    n_kv = kw // SWA_HEAD_DIM
    W = WINDOW
    grid_spec = pltpu.PrefetchScalarGridSpec(
        num_scalar_prefetch=1,
        grid=(L // W,),
        in_specs=[pl.BlockSpec((n_q, W), lambda n, s: (0, n)),
                  pl.BlockSpec((W, kw), lambda n, s: (n, 0)),
                  pl.BlockSpec((W, kw), lambda n, s: (jnp.maximum(n - 1, 0), 0)),
                  pl.BlockSpec((kw, W), lambda n, s: (0, n)),
                  pl.BlockSpec((kw, W), lambda n, s: (0, jnp.maximum(n - 1, 0)))],
        out_specs=pl.BlockSpec((W, n_q), lambda n, s: (n, 0)),
    )
    return pl.pallas_call(
        functools.partial(_swa_kernel, n_kv=n_kv),
        grid_spec=grid_spec,
        out_shape=jax.ShapeDtypeStruct((L, n_q), BF16),
        compiler_params=_cparams(1, 32),
        name="swa",
    )(sinks, qt, k, k, vt, vt)


def kernel(x, c, positions, w_in_ab, b_forget, ssm_lambda_re, ssm_lambda_im, ssm_log_dt, ssm_b_re, ssm_b_im, ssm_c_re, ssm_c_im, ssm_d, w_glu, b_glu, w_out_ab, w_in_c, attn_sinks, w_out_c, w_ada, b_ada, ln_mix_g, ln_mix_b, ln_ffn_g, ln_ffn_b, w_ffn_gate, w_ffn_up, w_ffn_down):
    B, L, D = x.shape
    assert B == 1
    x2 = x.reshape(L, D)
    mod = _ada(c, w_ada, b_ada)

    def mods(layer):
        return [mod[layer, :, k * D:(k + 1) * D] for k in range(6)]

    def vec(a):
        return a.reshape(1, -1)

    sh1, sc1, g1, sh2, sc2, g2 = mods(0)
    W = FOX_WIDTH
    w_in = w_in_ab[0]
    w_qku = jnp.concatenate([w_in[:, :2 * W], w_in[:, 3 * W + FOX_HEADS:]], 1).astype(BF16)
    w_vt = w_in[:, 2 * W:3 * W].T.astype(BF16)
    w_f = jnp.pad(w_in[:, 3 * W:3 * W + FOX_HEADS], ((0, 0), (0, LANES - FOX_HEADS))).astype(BF16)
    b_f = jnp.pad(b_forget[0], (0, LANES - FOX_HEADS)).reshape(1, LANES)
    tq = 512
    qa, ka, u, vt = _inproj0(x2, sc1, sh1, w_qku, w_vt, w_f, b_f, tq)
    a_out = _fox(qa, ka, vt, tq)

    T = SSM_CHUNK
    nrow = L // T
    nsub = nrow // SUBLANES
    width = u.shape[1]
    u_s = u.reshape(SUBLANES, nsub, T, width).transpose(2, 1, 0, 3).reshape(T, nrow, width)
    tables = _ssm_tables(ssm_lambda_re[0], ssm_lambda_im[0], ssm_log_dt[0], ssm_b_re[0],
                         ssm_b_im[0], ssm_c_re[0], ssm_c_im[0], ssm_d[0], nsub)
    y_s = _ssm(u_s, *tables)
    y = y_s.reshape(T, nsub, SUBLANES, width).transpose(2, 1, 0, 3).reshape(L, width)

    x2 = _mixout(a_out, y, x2, w_glu[0].astype(BF16), vec(b_glu[0]), w_out_ab[0].astype(BF16),
                 g1, vec(ln_mix_g[0]), vec(ln_mix_b[0]))
    wg, wu, wd = (w.astype(BF16) for w in (w_ffn_gate, w_ffn_up, w_ffn_down))
    x2 = _ffn(x2, sc2, sh2, wg, wu, wd, 0, g2, vec(ln_ffn_g[0]), vec(ln_ffn_b[0]))

    sh1, sc1, g1, sh2, sc2, g2 = mods(1)
    n_q = w_out_c.shape[1]
    n_k = (w_in_c.shape[2] - n_q) // 2
    w_c = w_in_c[0]
    a1 = _swa(*_inproj1(x2, sc1, sh1, positions, w_c[:, :n_q].T.astype(BF16),
                        w_c[:, n_q:n_q + n_k].astype(BF16), w_c[:, n_q + n_k:].T.astype(BF16)),
              attn_sinks[0])
    x2 = _mixout(a1, None, x2, None, None, w_out_c[0].astype(BF16), g1,
                 vec(ln_mix_g[1]), vec(ln_mix_b[1]))
    x2 = _ffn(x2, sc2, sh2, wg, wu, wd, 1, g2, vec(ln_ffn_g[1]), vec(ln_ffn_b[1]))
    return x2.reshape(B, L, D)
```

```python
import functools
import math

import jax
import jax.numpy as jnp
from jax import lax
from jax.experimental import pallas as pl
from jax.experimental.pallas import tpu as pltpu

F32 = jnp.float32
BF16 = jnp.bfloat16

FOX_HEADS = 8
FOX_HEAD_DIM = 128
FOX_WIDTH = FOX_HEADS * FOX_HEAD_DIM
SSM_GROUP = 16
SSM_STATE = 64
SWA_HEAD_DIM = 64
SWA_Q_PER_KV = 8
WINDOW = 128
ROT_DIM = SWA_HEAD_DIM // 4
ROPE_THETA = 500000.0
DEPTH = 2
DEEPNORM_ALPHA = (2 * DEPTH) ** 0.25
LN_EPS = 1e-5

LANES = 128
SUBLANES = 8
NEG_BIG = -1e30

SSM_CHUNK = 16
SSM_BLOCK_GROUPS = LANES // SSM_GROUP


def _cparams(n_axes, vmem_mb):
    return pltpu.CompilerParams(
        dimension_semantics=("arbitrary",) * n_axes,
        vmem_limit_bytes=vmem_mb * 1024 * 1024)


def _sigmoid(x):
    return 1.0 / (1.0 + jnp.exp(-x))


def _layer_norm(r, g, b):
    mu = jnp.mean(r, axis=-1, keepdims=True)
    d = r - mu
    var = jnp.mean(d * d, axis=-1, keepdims=True)
    return d * lax.rsqrt(var + LN_EPS) * g + b


def _dot(a, b):
    return jnp.dot(a, b, preferred_element_type=F32)


def _dot_nt(a, b):
    return lax.dot_general(a, b, (((1,), (1,)), ((), ())), preferred_element_type=F32)


def _ada_kernel(c_ref, w_ref, b_ref, o_ref):
    c = c_ref[...]
    sc = c * _sigmoid(c)
    o_ref[0] = jnp.dot(sc, w_ref[0], precision=lax.Precision.HIGHEST,
                       preferred_element_type=F32) + b_ref[0]


def _ada(c, w_ada, b_ada):
    depth, d, n = w_ada.shape
    tn = 1536
    c8 = jnp.broadcast_to(c, (SUBLANES, d))
    out = pl.pallas_call(
        _ada_kernel,
        grid=(depth, n // tn),
        in_specs=[pl.BlockSpec((SUBLANES, d), lambda l, j: (0, 0)),
                  pl.BlockSpec((1, d, tn), lambda l, j: (l, 0, j)),
                  pl.BlockSpec((1, 1, tn), lambda l, j: (l, 0, j))],
        out_specs=pl.BlockSpec((1, SUBLANES, tn), lambda l, j: (l, 0, j)),
        out_shape=jax.ShapeDtypeStruct((depth, SUBLANES, n), F32),
        compiler_params=_cparams(2, 40),
        name="ada",
    )(c8, w_ada, b_ada.reshape(depth, 1, n))
    return out[:, 0:1, :]


N_SPLIT = 3
ONES_ROWS = 16
LOG2E = 1.0 / math.log(2.0)


def _inproj0_kernel(x_ref, sc_ref, sh_ref, w_ref, wvt_ref, wf_ref, bf_ref, pq_ref, pk_ref,
                    cq_ref, ck_ref, qa_ref, ka_ref, u_ref, vt_ref, h_sc, carry_sc,
                    *, tm, scale):
    i = pl.program_id(0)
    j = pl.program_id(1)
    hd = FOX_HEAD_DIM

    def put_heads(dst_ref, val, off):
        for h in range(FOX_HEADS):
            dst_ref[:, 2 * h * hd + off:2 * h * hd + off + hd] = (
                val[:, h * hd:(h + 1) * hd].astype(BF16))

    @pl.when(j == 0)
    def _():
        hb = (x_ref[...] * (1.0 + sc_ref[...]) + sh_ref[...]).astype(BF16)
        h_sc[...] = hb
        z = _dot(hb, wf_ref[...]) + bf_ref[...]
        lf = jnp.minimum(z, 0.0) - jnp.log1p(jnp.exp(-jnp.abs(z)))

        def split(v):
            p1 = v.astype(BF16)
            r1 = v - p1.astype(F32)
            p2 = r1.astype(BF16)
            return p1, p2, (r1 - p2.astype(F32)).astype(BF16)

        row = lax.broadcasted_iota(jnp.int32, (tm, tm), 0)
        col = lax.broadcasted_iota(jnp.int32, (tm, tm), 1)
        tri = (col <= row).astype(BF16)
        cum = sum(_dot(tri, p) for p in split(lf))

        @pl.when(i == 0)
        def _():
            carry_sc[...] = jnp.zeros_like(carry_sc)

        f = cum + carry_sc[...]
        carry_sc[...] = f[tm - 1:tm, :]
        fp = jnp.concatenate(split(f * LOG2E), axis=1)
        put_heads(qa_ref, _dot(fp, pq_ref[...]) + cq_ref[...], hd)
        put_heads(ka_ref, _dot(fp, pk_ref[...]) + ck_ref[...], hd)
        put_heads(qa_ref, _dot(hb, w_ref[...]) * scale, 0)

    @pl.when(j == 1)
    def _():
        put_heads(ka_ref, _dot(h_sc[...], w_ref[...]), 0)

    @pl.when(j == 2)
    def _():
        u_ref[...] = _dot(h_sc[...], w_ref[...]).astype(BF16)

    @pl.when(j == 3)
    def _():
        vt = _dot_nt(wvt_ref[...], h_sc[...])
        vt_ref[:, :, 0:hd, :] = vt.astype(BF16).reshape(FOX_HEADS, 1, hd, tm)
        vt_ref[:, :, hd:, :] = jnp.ones((FOX_HEADS, 1, ONES_ROWS, tm), BF16)


def _inproj0(x, sc, sh, w_qku, w_vt, wf, bf, tm):
    L, d = x.shape
    W = FOX_WIDTH
    hd = FOX_HEAD_DIM
    piece = jnp.arange(N_SPLIT)
    head = jnp.arange(FOX_HEADS)
    rows = (piece[:, None] * LANES + head[None, :]).reshape(-1)
    cols_k = (head[None, :] * hd + piece[:, None]).reshape(-1)
    cols_q = cols_k + N_SPLIT
    pq = jnp.zeros((N_SPLIT * LANES, W), F32).at[rows, cols_q].set(1.0).astype(BF16)
    pk = jnp.zeros((N_SPLIT * LANES, W), F32).at[rows, cols_k].set(-1.0).astype(BF16)
    lane = jnp.arange(W) % hd
    cq = (lane < N_SPLIT).astype(F32).reshape(1, W)
    ck = ((lane >= N_SPLIT) & (lane < 2 * N_SPLIT)).astype(F32).reshape(1, W)
    kern = functools.partial(_inproj0_kernel, tm=tm, scale=LOG2E / math.sqrt(hd))
    hv = hd + ONES_ROWS
    fix = lambda i, j: (0, 0)
    row = lambda i, j: (i, 0)
    return pl.pallas_call(
        kern,
        grid=(L // tm, 4),
        in_specs=[pl.BlockSpec((tm, d), row),
                  pl.BlockSpec((1, d), fix),
                  pl.BlockSpec((1, d), fix),
                  pl.BlockSpec((d, W), lambda i, j: (0, jnp.minimum(j, 2))),
                  pl.BlockSpec((W, d), fix),
                  pl.BlockSpec((d, LANES), fix),
                  pl.BlockSpec((1, LANES), fix),
                  pl.BlockSpec((N_SPLIT * LANES, W), fix),
                  pl.BlockSpec((N_SPLIT * LANES, W), fix),
                  pl.BlockSpec((1, W), fix),
                  pl.BlockSpec((1, W), fix)],
        out_specs=[pl.BlockSpec((tm, 2 * W), row),
                   pl.BlockSpec((tm, 2 * W), row),
                   pl.BlockSpec((tm, W), row),
                   pl.BlockSpec((FOX_HEADS, 1, hv, tm), lambda i, j: (0, i, 0, 0))],
        out_shape=[jax.ShapeDtypeStruct((L, 2 * W), BF16),
                   jax.ShapeDtypeStruct((L, 2 * W), BF16),
                   jax.ShapeDtypeStruct((L, W), BF16),
                   jax.ShapeDtypeStruct((FOX_HEADS, L // tm, hv, tm), BF16)],
        scratch_shapes=[pltpu.VMEM((tm, d), BF16), pltpu.VMEM((1, LANES), F32)],
        compiler_params=_cparams(2, 48),
        name="inproj0",
    )(x, sc, sh, w_qku, w_vt, wf, bf, pq, pk, cq, ck)


M_INIT = -1e29


def _fox_kernel(qa_ref, ka_ref, vt_ref, o_ref, s0, s1, p0, p1, al0, al1, mb0, mb1, m_sc,
                acc_sc, mask_sc, *, tq, tk):
    first = jnp.logical_and(pl.program_id(0) == 0, pl.program_id(1) == 0)
    i = pl.program_id(1)
    last = 2 * i + 1

    @pl.when(first)
    def _():
        key = lax.broadcasted_iota(jnp.int32, (tk, tq), 0)
        qry = lax.broadcasted_iota(jnp.int32, (tk, tq), 1)
        mask_sc[0] = jnp.zeros((tk, tq), F32)
        mask_sc[1] = jnp.where(key <= qry, 0.0, NEG_BIG)
        mask_sc[2] = jnp.where(key + tk <= qry, 0.0, NEG_BIG)

    q = qa_ref[...]
    hd = o_ref.shape[1]
    s1[...] = jnp.full_like(s1, NEG_BIG)
    mb1[...] = jnp.full_like(mb1, NEG_BIG)
    p0[...] = jnp.zeros_like(p0)
    al0[...] = jnp.ones_like(al0)
    m_sc[...] = jnp.full_like(m_sc, M_INIT)
    acc_sc[...] = jnp.zeros_like(acc_sc)

    def scores(t, s_ref, mb_ref, masked):
        start = pl.multiple_of(t * tk, tk)
        s = _dot_nt(ka_ref[pl.ds(start, tk), :], q)
        if masked:
            s = s + mask_sc[jnp.clip(t - last + 2, 0, 2)]
        s_ref[...] = s
        mb_ref[...] = jnp.max(s, axis=0, keepdims=True)

    def softmax(s_ref, mb_ref, p_ref, al_ref):
        m_prev = m_sc[...]
        m_new = jnp.maximum(m_prev, mb_ref[...])
        m_sc[...] = m_new
        al_ref[...] = jnp.exp2(m_prev - m_new)
        p_ref[...] = jnp.exp2(s_ref[...] - m_new).astype(BF16)

    def values(t, p_ref, al_ref):
        blk = jnp.maximum(t, 0)
        acc_sc[...] = al_ref[...] * acc_sc[...] + _dot(vt_ref[0, blk], p_ref[...])

    def pair(u, carry, masked):
        t = 2 * u
        scores(t, s0, mb0, masked)
        softmax(s1, mb1, p1, al1)
        values(t - 2, p0, al0)
        scores(t + 1, s1, mb1, masked)
        softmax(s0, mb0, p0, al0)
        values(t - 1, p1, al1)
        return carry

    def quad(v, carry):
        pair(2 * v, carry, False)
        return pair(2 * v + 1, carry, False)

    n_quad = i // 2
    lax.fori_loop(0, n_quad, quad, 0)
    lax.fori_loop(2 * n_quad, i + 1, functools.partial(pair, masked=True), 0)
    softmax(s1, mb1, p1, al1)
    values(last - 1, p0, al0)
    values(last, p1, al1)

    acc = acc_sc[...]
    o_ref[...] = (acc[0:hd] / acc[hd:hd + 1]).T.astype(BF16)


def _fox(qa, ka, vt, tk):
    L = qa.shape[0]
    hd = FOX_HEAD_DIM
    hv = hd + ONES_ROWS
    tq = 2 * tk
    assert vt.shape == (FOX_HEADS, L // tk, hv, tk)
    row = pltpu.VMEM((1, tq), F32)
    return pl.pallas_call(
        functools.partial(_fox_kernel, tq=tq, tk=tk),
        grid=(FOX_HEADS, L // tq),
        in_specs=[pl.BlockSpec((tq, 2 * hd), lambda h, i: (i, h)),
                  pl.BlockSpec((L, 2 * hd), lambda h, i: (0, h)),
                  pl.BlockSpec((1, L // tk, hv, tk), lambda h, i: (h, 0, 0, 0))],
        out_specs=pl.BlockSpec((tq, hd), lambda h, i: (i, h)),
        out_shape=jax.ShapeDtypeStruct((L, FOX_WIDTH), BF16),
        scratch_shapes=[pltpu.VMEM((tk, tq), F32), pltpu.VMEM((tk, tq), F32),
                        pltpu.VMEM((tk, tq), BF16), pltpu.VMEM((tk, tq), BF16),
                        row, row, row, row, row,
                        pltpu.VMEM((hv, tq), F32), pltpu.VMEM((3, tk, tq), F32)],
        compiler_params=_cparams(2, 48),
        name="fox",
    )(qa, ka, vt)


def _taps_kernel(cre_ref, cim_ref, pre_ref, pim_ref, bre_ref, bim_ref, o_ref):
    T, H, P = SSM_CHUNK, SSM_GROUP, SSM_STATE
    nt = (((1,), (1,)), ((), ()))
    for g in range(SSM_BLOCK_GROUPS):
        cre = cre_ref[g][None]
        cim = cim_ref[g][None]
        pre = pre_ref[g][:, None, :]
        pim = pim_ref[g][:, None, :]
        wre = (cre * pre - cim * pim).reshape(T * H, P)
        wim = (cre * pim + cim * pre).reshape(T * H, P)
        k = (lax.dot_general(bre_ref[g], wre, nt, precision=lax.Precision.HIGHEST,
                             preferred_element_type=F32)
             - lax.dot_general(bim_ref[g], wim, nt, precision=lax.Precision.HIGHEST,
                               preferred_element_type=F32))
        o_ref[g] = k


def _taps(c_re, c_im, pw_re, pw_im, bbt_re, bbt_im):
    G = c_re.shape[0]
    gb = SSM_BLOCK_GROUPS
    T, H, P = SSM_CHUNK, SSM_GROUP, SSM_STATE
    return pl.pallas_call(
        _taps_kernel,
        grid=(G // gb,),
        in_specs=[pl.BlockSpec((gb, H, P), lambda b: (b, 0, 0)),
                  pl.BlockSpec((gb, H, P), lambda b: (b, 0, 0)),
                  pl.BlockSpec((gb, T, P), lambda b: (b, 0, 0)),
                  pl.BlockSpec((gb, T, P), lambda b: (b, 0, 0)),
                  pl.BlockSpec((gb, H, P), lambda b: (b, 0, 0)),
                  pl.BlockSpec((gb, H, P), lambda b: (b, 0, 0))],
        out_specs=pl.BlockSpec((gb, H, T * H), lambda b: (b, 0, 0)),
        out_shape=jax.ShapeDtypeStruct((G, H, T * H), F32),
        compiler_params=_cparams(1, 32),
        name="taps",
    )(c_re, c_im, pw_re, pw_im, bbt_re, bbt_im)


def _ssm_kernel(u_ref, base_ref, w2r_ref, w2i_ref, v3r_ref, v3i_ref, rs_ref, r3_ref,
                ar_ref, ai_ref, pr_ref, pi_ref, a2r_ref, a2i_ref, o_ref,
                ucat_sc, s_sc, x_sc, m1_sc, m2_sc, m3_sc, *, nrow):
    tp = pl.program_id(1)
    T = SSM_CHUNK
    half = x_sc.shape[1] // 2
    ngrp = nrow // SUBLANES
    lg_h = SSM_GROUP.bit_length() - 1
    lg_p = SSM_STATE.bit_length() - 1

    def same_group(shape, row_shift, col_and, col_shift):
        r = lax.shift_right_logical(lax.broadcasted_iota(jnp.int32, shape, 0), row_shift)
        c = lax.broadcasted_iota(jnp.int32, shape, 1)
        c = lax.shift_right_logical(jnp.bitwise_and(c, col_and), col_shift)
        return (r == c).astype(F32)

    @pl.when(tp == 0)
    def _():
        for s in range(T):
            ucat_sc[:, s * LANES:(s + 1) * LANES] = u_ref[s]
        mask3 = same_group((LANES, half), lg_h, half - 1, lg_p)
        for v_ref, lo in ((v3r_ref, 0), (v3i_ref, half)):
            e = _dot(v_ref[0].astype(BF16), r3_ref[...])
            for s in range(T):
                m3_sc[s * LANES:(s + 1) * LANES, lo:lo + half] = (
                    e[s * LANES:(s + 1) * LANES] * mask3).astype(BF16)
        s_sc[...] = _dot(ucat_sc[...], m3_sc[...])
        ar = ar_ref[0]
        ai = ai_ref[0]

        def step(r, carry):
            xr, xi = carry
            row = pl.multiple_of(r * SUBLANES, SUBLANES)
            x_sc[pl.ds(row, SUBLANES), 0:half] = xr
            x_sc[pl.ds(row, SUBLANES), half:2 * half] = xi
            sr = s_sc[pl.ds(row, SUBLANES), 0:half]
            si = s_sc[pl.ds(row, SUBLANES), half:2 * half]
            return ar * xr - ai * xi + sr, ar * xi + ai * xr + si

        zero = jnp.zeros((SUBLANES, half), F32)
        er, ei = lax.fori_loop(0, ngrp, step, (zero, zero))

        a2r = a2r_ref[0]
        a2i = a2i_ref[0]
        sub = lax.broadcasted_iota(jnp.int32, (SUBLANES, half), 0)
        zr = jnp.zeros((1, half), F32)
        zi = jnp.zeros((1, half), F32)
        zr_all = zero
        zi_all = zero
        for jj in range(1, SUBLANES):
            nzr = a2r * zr - a2i * zi + er[jj - 1:jj]
            nzi = a2r * zi + a2i * zr + ei[jj - 1:jj]
            zr, zi = nzr, nzi
            zr_all = jnp.where(sub == jj, zr, zr_all)
            zi_all = jnp.where(sub == jj, zi, zi_all)

        def fix(r, carry):
            row = pl.multiple_of(r * SUBLANES, SUBLANES)
            pr = pr_ref[0, pl.ds(r, 1), :]
            pi = pi_ref[0, pl.ds(r, 1), :]
            x_sc[pl.ds(row, SUBLANES), 0:half] = (
                x_sc[pl.ds(row, SUBLANES), 0:half] + pr * zr_all - pi * zi_all)
            x_sc[pl.ds(row, SUBLANES), half:2 * half] = (
                x_sc[pl.ds(row, SUBLANES), half:2 * half] + pr * zi_all + pi * zr_all)
            return carry

        lax.fori_loop(0, ngrp, fix, 0)

    base = base_ref[0].astype(BF16)
    mask1 = same_group((LANES, 2 * LANES), lg_h, LANES - 1, lg_h)
    for s in range(T):
        m1_sc[s * LANES:(s + 1) * LANES, :] = (_dot(base, rs_ref[s]) * mask1).astype(BF16)
    mask2 = same_group((half, 2 * LANES), lg_p, LANES - 1, lg_h)
    m2_sc[0:half, :] = (_dot(w2r_ref[0].astype(BF16), rs_ref[0]) * mask2).astype(BF16)
    m2_sc[half:2 * half, :] = (-_dot(w2i_ref[0].astype(BF16), rs_ref[0]) * mask2).astype(BF16)

    y = _dot(ucat_sc[...], m1_sc[...]) + _dot(x_sc[...].astype(BF16), m2_sc[...])
    o_ref[0] = y[:, 0:LANES].astype(BF16)
    o_ref[1] = y[:, LANES:2 * LANES].astype(BF16)


def _ssm(u_s, base, w2_re, w2_im, v3_re, v3_im, rs, r3, a_re, a_im, p_re, p_im, a2_re, a2_im):
    T, nrow, width = u_s.shape
    nb = width // LANES
    half = r3.shape[1]
    nstate = 2 * half
    ngrp = nrow // SUBLANES
    th = rs.shape[1]
    P = r3.shape[0]
    kern = functools.partial(_ssm_kernel, nrow=nrow)
    return pl.pallas_call(
        kern,
        grid=(nb, T // 2),
        in_specs=[pl.BlockSpec((T, nrow, LANES), lambda b, t: (0, 0, b)),
                  pl.BlockSpec((1, LANES, th), lambda b, t: (b, 0, 0)),
                  pl.BlockSpec((1, half, th), lambda b, t: (b, 0, 0)),
                  pl.BlockSpec((1, half, th), lambda b, t: (b, 0, 0)),
                  pl.BlockSpec((1, T * LANES, P), lambda b, t: (b, 0, 0)),
                  pl.BlockSpec((1, T * LANES, P), lambda b, t: (b, 0, 0)),
                  pl.BlockSpec((T, th, 2 * LANES), lambda b, t: (0, 0, t)),
                  pl.BlockSpec((P, half), lambda b, t: (0, 0)),
                  pl.BlockSpec((1, 1, half), lambda b, t: (b, 0, 0)),
                  pl.BlockSpec((1, 1, half), lambda b, t: (b, 0, 0)),
                  pl.BlockSpec((1, ngrp, half), lambda b, t: (b, 0, 0)),
                  pl.BlockSpec((1, ngrp, half), lambda b, t: (b, 0, 0)),
                  pl.BlockSpec((1, 1, half), lambda b, t: (b, 0, 0)),
                  pl.BlockSpec((1, 1, half), lambda b, t: (b, 0, 0))],
        out_specs=pl.BlockSpec((2, nrow, LANES), lambda b, t: (t, 0, b)),
        out_shape=jax.ShapeDtypeStruct((T, nrow, width), BF16),
        scratch_shapes=[pltpu.VMEM((nrow, T * LANES), BF16),
                        pltpu.VMEM((nrow, nstate), F32),
                        pltpu.VMEM((nrow, nstate), F32),
                        pltpu.VMEM((T * LANES, 2 * LANES), BF16),
                        pltpu.VMEM((nstate, 2 * LANES), BF16),
                        pltpu.VMEM((T * LANES, nstate), BF16)],
        compiler_params=_cparams(2, 48),
        name="ssm",
    )(u_s, base, w2_re, w2_im, v3_re, v3_im, rs, r3, a_re, a_im, p_re, p_im, a2_re, a2_im)


def _ssm_tables(lam_re, lam_im, log_dt, b_re, b_im, c_re, c_im, d_skip, ngrp):
    G, P = lam_re.shape
    H = SSM_GROUP
    T = SSM_CHUNK
    gb = SSM_BLOCK_GROUPS
    nb = G // gb
    dt = jnp.exp(log_dt)[:, None]
    mag = jnp.exp(lam_re * dt)
    lb_re = mag * jnp.cos(lam_im * dt)
    lb_im = mag * jnp.sin(lam_im * dt)
    den = lam_re * lam_re + lam_im * lam_im
    nr = lb_re - 1.0
    q_re = (nr * lam_re + lb_im * lam_im) / den
    q_im = (lb_im * lam_re - nr * lam_im) / den
    bb_re = q_re[..., None] * b_re - q_im[..., None] * b_im
    bb_im = q_re[..., None] * b_im + q_im[..., None] * b_re

    def cmul(ar, ai, br, bi):
        return ar * br - ai * bi, ar * bi + ai * br

    def powers(ar, ai, n):
        pr, pi = jnp.ones_like(ar)[None], jnp.zeros_like(ai)[None]
        sr, si = ar, ai
        while pr.shape[0] < n:
            nr_, ni_ = cmul(pr, pi, sr[None], si[None])
            pr = jnp.concatenate([pr, nr_], 0)
            pi = jnp.concatenate([pi, ni_], 0)
            sr, si = cmul(sr, si, sr, si)
        return pr[:n], pi[:n], sr, si

    pw_re, pw_im, _, _ = powers(lb_re, lb_im, 2 * T)
    aT_re, aT_im = pw_re[T], pw_im[T]
    pT_re, pT_im, a2_re, a2_im = powers(aT_re, aT_im, ngrp)

    bbt_re = jnp.transpose(bb_re, (0, 2, 1))
    bbt_im = jnp.transpose(bb_im, (0, 2, 1))
    taps = _taps(c_re, c_im,
                 jnp.transpose(pw_re[:T], (1, 0, 2)), jnp.transpose(pw_im[:T], (1, 0, 2)),
                 bbt_re, bbt_im)
    jj = jnp.arange(H)
    taps = taps.at[:, jj, jj].add(d_skip)
    base = taps.reshape(nb, gb * H, T * H)

    er = pw_re[:T][::-1][:, :, None, :]
    ei = pw_im[:T][::-1][:, :, None, :]
    v_re = er * bbt_re[None] - ei * bbt_im[None]
    v_im = er * bbt_im[None] + ei * bbt_re[None]

    def rows3(v):
        v = v.reshape(T, nb, gb * H, P)
        return jnp.transpose(v, (1, 0, 2, 3)).reshape(nb, T * LANES, P)

    fr = jnp.transpose(pw_re[1:T + 1], (1, 2, 0))[..., None]
    fi = jnp.transpose(pw_im[1:T + 1], (1, 2, 0))[..., None]
    ct_re = jnp.transpose(c_re, (0, 2, 1))[:, :, None, :]
    ct_im = jnp.transpose(c_im, (0, 2, 1))[:, :, None, :]
    w2_re = (ct_re * fr - ct_im * fi).reshape(nb, gb * P, T * H)
    w2_im = (ct_re * fi + ct_im * fr).reshape(nb, gb * P, T * H)

    rr = jnp.arange(T * H)
    cc = jnp.arange(T * LANES)
    same_i = (rr % H)[None, :, None] == (cc % H)[None, None, :]
    shifted = ((rr // H)[None, :, None] + jnp.arange(T)[:, None, None]
               == (cc // LANES)[None, None, :])
    rs = (same_i & shifted).astype(BF16)
    r3 = (jnp.arange(P)[:, None] == (jnp.arange(gb * P) % P)[None, :]).astype(BF16)

    def lanes(a):
        lead = a.shape[:-2]
        a = a.reshape(lead + (nb, gb * P))
        return jnp.moveaxis(a, -2, 0)

    a_re = lanes(aT_re)[:, None, :]
    a_im = lanes(aT_im)[:, None, :]
    p_re = lanes(pT_re)
    p_im = lanes(pT_im)
    a2_re = lanes(a2_re)[:, None, :]
    a2_im = lanes(a2_im)[:, None, :]
    return (base, w2_re, w2_im, rows3(v_re), rows3(v_im), rs, r3,
            a_re, a_im, p_re, p_im, a2_re, a2_im)


def _mixout_kernel(*refs, glu):
    if glu:
        (a_ref, y_ref, x_ref, wglu_ref, bglu_ref, wo_ref, g_ref, lng_ref, lnb_ref,
         o_ref) = refs
        y = y_ref[...].astype(F32)
        s = y * (0.5 * (1.0 + jnp.tanh(math.sqrt(2.0 / math.pi)
                                       * (y + 0.044715 * (y * y * y)))))
        z = _dot(s.astype(BF16), wglu_ref[...]) + bglu_ref[...]
        s2 = (s * _sigmoid(z)).astype(BF16)
        wa = a_ref.shape[1]
        mix = _dot(a_ref[...], wo_ref[0:wa, :]) + _dot(s2, wo_ref[wa:, :])
    else:
        a_ref, x_ref, wo_ref, g_ref, lng_ref, lnb_ref, o_ref = refs
        mix = _dot(a_ref[...], wo_ref[...])
    r = DEEPNORM_ALPHA * x_ref[...] + g_ref[...] * mix
    o_ref[...] = _layer_norm(r, lng_ref[...], lnb_ref[...])


def _mixout(a, y, x, wglu, bglu, wo, g, lng, lnb):
    L, d = x.shape
    tm = 512
    glu = y is not None
    row = lambda i: (i, 0)
    fix = lambda i: (0, 0)
    vec = pl.BlockSpec((1, d), fix)
    if glu:
        ws = y.shape[1]
        args = (a, y, x, wglu, bglu, wo, g, lng, lnb)
        in_specs = [pl.BlockSpec((tm, a.shape[1]), row), pl.BlockSpec((tm, ws), row),
                    pl.BlockSpec((tm, d), row), pl.BlockSpec((ws, ws), fix),
                    pl.BlockSpec((1, ws), fix), pl.BlockSpec(wo.shape, fix), vec, vec, vec]
    else:
        args = (a, x, wo, g, lng, lnb)
        in_specs = [pl.BlockSpec((tm, a.shape[1]), row), pl.BlockSpec((tm, d), row),
                    pl.BlockSpec(wo.shape, fix), vec, vec, vec]
    return pl.pallas_call(
        functools.partial(_mixout_kernel, glu=glu),
        grid=(L // tm,),
        in_specs=in_specs,
        out_specs=pl.BlockSpec((tm, d), row),
        out_shape=jax.ShapeDtypeStruct((L, d), F32),
        compiler_params=_cparams(1, 48),
        name="mixout_glu" if glu else "mixout",
    )(*args)


def _ffn_kernel(x_ref, sc_ref, sh_ref, wg_ref, wu_ref, wd_ref, g_ref, lng_ref, lnb_ref,
                o_ref, h_sc):
    j = pl.program_id(1)

    @pl.when(j == 0)
    def _():
        h_sc[...] = (x_ref[...] * (1.0 + sc_ref[...]) + sh_ref[...]).astype(BF16)
        o_ref[...] = jnp.zeros_like(o_ref)

    h = h_sc[...]
    gt = _dot(h, wg_ref[0])
    ut = _dot(h, wu_ref[0])
    act = (gt * _sigmoid(gt) * ut).astype(BF16)
    o_ref[...] += _dot(act, wd_ref[0])

    @pl.when(j == pl.num_programs(1) - 1)
    def _():
        r = DEEPNORM_ALPHA * x_ref[...] + g_ref[...] * o_ref[...]
        o_ref[...] = _layer_norm(r, lng_ref[...], lnb_ref[...])


def _ffn(x, sc, sh, wg, wu, wd, layer, g, lng, lnb):
    L, d = x.shape
    dff = wg.shape[2]
    tm, tf = 1024, 256
    row = lambda i, j: (i, 0)
    vec = pl.BlockSpec((1, d), lambda i, j: (0, 0))
    return pl.pallas_call(
        _ffn_kernel,
        grid=(L // tm, dff // tf),
        in_specs=[pl.BlockSpec((tm, d), row), vec, vec,
                  pl.BlockSpec((1, d, tf), lambda i, j: (layer, 0, j)),
                  pl.BlockSpec((1, d, tf), lambda i, j: (layer, 0, j)),
                  pl.BlockSpec((1, tf, d), lambda i, j: (layer, j, 0)),
                  vec, vec, vec],
        out_specs=pl.BlockSpec((tm, d), row),
        out_shape=jax.ShapeDtypeStruct((L, d), F32),
        scratch_shapes=[pltpu.VMEM((tm, d), BF16)],
        compiler_params=_cparams(2, 58),
        name="ffn",
    )(x, sc, sh, wg, wu, wd, g, lng, lnb)


def _inproj1_kernel(x_ref, sc_ref, sh_ref, posc_ref, posr_ref, freq_ref, slo_ref, shi_ref,
                    fcol_ref, wqt_ref, wk_ref, wvt_ref, qt_ref, k_ref, vt_ref, h_sc,
                    *, nq, scale):
    j = pl.program_id(1)
    half = ROT_DIM // 2
    dh = SWA_HEAD_DIM

    @pl.when(j == 0)
    def _():
        h_sc[...] = (x_ref[...] * (1.0 + sc_ref[...]) + sh_ref[...]).astype(BF16)

    @pl.when(j < nq)
    def _():
        r = _dot_nt(wqt_ref[...], h_sc[...])
        ang = fcol_ref[...] * posr_ref[...]
        cs = jnp.cos(ang)
        sn = jnp.sin(ang)
        for h in range(r.shape[0] // dh):
            b = h * dh
            x1 = r[b:b + half]
            x2 = r[b + half:b + ROT_DIM]
            rot = jnp.concatenate([x1 * cs - x2 * sn, x2 * cs + x1 * sn], axis=0)
            qt_ref[b:b + ROT_DIM, :] = (rot * scale).astype(BF16)
            qt_ref[b + ROT_DIM:b + dh, :] = (r[b + ROT_DIM:b + dh] * scale).astype(BF16)

    @pl.when(j == nq)
    def _():
        hb = h_sc[...]
        kk = _dot(hb, wk_ref[...])
        ang = posc_ref[...] * freq_ref[...]
        cs = jnp.cos(ang)
        sn = jnp.sin(ang)
        sin_lo = sn * slo_ref[...]
        sin_hi = sn * shi_ref[...]
        for c in range(kk.shape[1] // LANES):
            xc = kk[:, c * LANES:(c + 1) * LANES]
            k_ref[:, c * LANES:(c + 1) * LANES] = (
                xc * cs + pltpu.roll(xc, LANES - half, 1) * sin_lo
                + pltpu.roll(xc, half, 1) * sin_hi).astype(BF16)
        vt_ref[...] = _dot_nt(wvt_ref[...], hb).astype(BF16)


def _inproj1(x, sc, sh, pos, w_qt, w_k, w_vt):
    L, d = x.shape
    n_q = w_qt.shape[0]
    n_k = w_k.shape[1]
    tm, tn = 512, 512
    nq = n_q // tn
    half = ROT_DIM // 2
    inv_freq = jnp.power(jnp.float32(ROPE_THETA), -jnp.arange(half, dtype=F32) * (2.0 / ROT_DIM))
    dlane = jnp.arange(LANES) % SWA_HEAD_DIM
    freq = jnp.where(dlane < ROT_DIM, inv_freq[dlane % half], 0.0).reshape(1, LANES)
    slo = jnp.where(dlane < half, -1.0, 0.0).astype(F32).reshape(1, LANES)
    shi = jnp.where((dlane >= half) & (dlane < ROT_DIM), 1.0, 0.0).astype(F32).reshape(1, LANES)
    posf = pos.astype(F32)
    kern = functools.partial(_inproj1_kernel, nq=nq, scale=LOG2E / math.sqrt(SWA_HEAD_DIM))
    fix = lambda i, j: (0, 0)
    qrow = lambda i, j: (jnp.minimum(j, nq - 1), 0)
    return pl.pallas_call(
        kern,
        grid=(L // tm, nq + 1),
        in_specs=[pl.BlockSpec((tm, d), lambda i, j: (i, 0)),
                  pl.BlockSpec((1, d), fix), pl.BlockSpec((1, d), fix),
                  pl.BlockSpec((tm, 1), lambda i, j: (i, 0)),
                  pl.BlockSpec((1, tm), lambda i, j: (0, i)),
                  pl.BlockSpec((1, LANES), fix), pl.BlockSpec((1, LANES), fix),
                  pl.BlockSpec((1, LANES), fix), pl.BlockSpec((half, 1), fix),
                  pl.BlockSpec((tn, d), qrow),
                  pl.BlockSpec((d, n_k), fix),
                  pl.BlockSpec((n_k, d), fix)],
        out_specs=[pl.BlockSpec((tn, tm), lambda i, j: (jnp.minimum(j, nq - 1), i)),
                   pl.BlockSpec((tm, n_k), lambda i, j: (i, 0)),
                   pl.BlockSpec((n_k, tm), lambda i, j: (0, i))],
        out_shape=[jax.ShapeDtypeStruct((n_q, L), BF16),
                   jax.ShapeDtypeStruct((L, n_k), BF16),
                   jax.ShapeDtypeStruct((n_k, L), BF16)],
        scratch_shapes=[pltpu.VMEM((tm, d), BF16)],
        compiler_params=_cparams(2, 40),
        name="inproj1",
    )(x, sc, sh, posf.reshape(L, 1), posf.reshape(1, L), freq, slo, shi,
      inv_freq.reshape(half, 1), w_qt, w_k, w_vt)


def _swa_kernel(sink_ref, qt_ref, kc_ref, kp_ref, vtc_ref, vtp_ref, o_ref, *, n_kv):
    n = pl.program_id(0)
    W = WINDOW
    dh = SWA_HEAD_DIM
    G = SWA_Q_PER_KV
    kj = lax.broadcasted_iota(jnp.int32, (2 * W, W), 0)
    qi = lax.broadcasted_iota(jnp.int32, (2 * W, W), 1)
    lo_key = jnp.where(n > 0, qi, W - 1)
    band = jnp.logical_and(kj > lo_key, kj <= qi + W)
    mask1 = jnp.where(band, 0.0, NEG_BIG)
    mask = jnp.concatenate([mask1] * G, axis=1)
    for kvh in range(n_kv):
        ksl = slice(kvh * dh, (kvh + 1) * dh)
        kk = jnp.concatenate([kp_ref[:, ksl], kc_ref[:, ksl]], axis=0)
        vv = jnp.concatenate([vtp_ref[ksl, :], vtc_ref[ksl, :]], axis=1)
        q = jnp.concatenate([qt_ref[(kvh * G + g) * dh:(kvh * G + g + 1) * dh, :]
                             for g in range(G)], axis=1)
        s = _dot(kk, q) + mask
        sink = jnp.concatenate([jnp.full((1, W), sink_ref[kvh * G + g] * LOG2E, F32)
                                for g in range(G)], axis=1)
        m = jnp.maximum(jnp.max(s, axis=0, keepdims=True), sink)
        p = jnp.exp2(s - m)
        den = jnp.sum(p, axis=0, keepdims=True) + jnp.exp2(sink - m)
        o = _dot(vv, p.astype(BF16)) / den
        for gp in range(G // 2):
            two = jnp.concatenate([o[:, (2 * gp) * W:(2 * gp + 1) * W],
                                   o[:, (2 * gp + 1) * W:(2 * gp + 2) * W]], axis=0)
            lo = (kvh * G + 2 * gp) * dh
            o_ref[:, lo:lo + 2 * dh] = two.T.astype(BF16)


def _swa(qt, k, vt, sinks):
    n_q, L = qt.shape
    kw = k.shape[1]
    n_kv = kw // SWA_HEAD_DIM
    W = WINDOW
    grid_spec = pltpu.PrefetchScalarGridSpec(
        num_scalar_prefetch=1,
        grid=(L // W,),
        in_specs=[pl.BlockSpec((n_q, W), lambda n, s: (0, n)),
                  pl.BlockSpec((W, kw), lambda n, s: (n, 0)),
                  pl.BlockSpec((W, kw), lambda n, s: (jnp.maximum(n - 1, 0), 0)),
                  pl.BlockSpec((kw, W), lambda n, s: (0, n)),
                  pl.BlockSpec((kw, W), lambda n, s: (0, jnp.maximum(n - 1, 0)))],
        out_specs=pl.BlockSpec((W, n_q), lambda n, s: (n, 0)),
    )
    return pl.pallas_call(
        functools.partial(_swa_kernel, n_kv=n_kv),
        grid_spec=grid_spec,
        out_shape=jax.ShapeDtypeStruct((L, n_q), BF16),
        compiler_params=_cparams(1, 32),
        name="swa",
    )(sinks, qt, k, k, vt, vt)


def kernel(x, c, positions, w_in_ab, b_forget, ssm_lambda_re, ssm_lambda_im, ssm_log_dt, ssm_b_re, ssm_b_im, ssm_c_re, ssm_c_im, ssm_d, w_glu, b_glu, w_out_ab, w_in_c, attn_sinks, w_out_c, w_ada, b_ada, ln_mix_g, ln_mix_b, ln_ffn_g, ln_ffn_b, w_ffn_gate, w_ffn_up, w_ffn_down):
    B, L, D = x.shape
    assert B == 1
    x2 = x.reshape(L, D)
    mod = _ada(c, w_ada, b_ada)

    def mods(layer):
        return [mod[layer, :, k * D:(k + 1) * D] for k in range(6)]

    def vec(a):
        return a.reshape(1, -1)

    sh1, sc1, g1, sh2, sc2, g2 = mods(0)
    W = FOX_WIDTH
    w_in = w_in_ab[0]
    w_qku = jnp.concatenate([w_in[:, :2 * W], w_in[:, 3 * W + FOX_HEADS:]], 1).astype(BF16)
    w_vt = w_in[:, 2 * W:3 * W].T.astype(BF16)
    w_f = jnp.pad(w_in[:, 3 * W:3 * W + FOX_HEADS], ((0, 0), (0, LANES - FOX_HEADS))).astype(BF16)
    b_f = jnp.pad(b_forget[0], (0, LANES - FOX_HEADS)).reshape(1, LANES)
    tq = 512
    qa, ka, u, vt = _inproj0(x2, sc1, sh1, w_qku, w_vt, w_f, b_f, tq)
    a_out = _fox(qa, ka, vt, tq)

    T = SSM_CHUNK
    nrow = L // T
    nsub = nrow // SUBLANES
    width = u.shape[1]
    u_s = u.reshape(SUBLANES, nsub, T, width).transpose(2, 1, 0, 3).reshape(T, nrow, width)
    tables = _ssm_tables(ssm_lambda_re[0], ssm_lambda_im[0], ssm_log_dt[0], ssm_b_re[0],
                         ssm_b_im[0], ssm_c_re[0], ssm_c_im[0], ssm_d[0], nsub)
    y_s = _ssm(u_s, *tables)
    y = y_s.reshape(T, nsub, SUBLANES, width).transpose(2, 1, 0, 3).reshape(L, width)

    x2 = _mixout(a_out, y, x2, w_glu[0].astype(BF16), vec(b_glu[0]), w_out_ab[0].astype(BF16),
                 g1, vec(ln_mix_g[0]), vec(ln_mix_b[0]))
    wg, wu, wd = (w.astype(BF16) for w in (w_ffn_gate, w_ffn_up, w_ffn_down))
    x2 = _ffn(x2, sc2, sh2, wg, wu, wd, 0, g2, vec(ln_ffn_g[0]), vec(ln_ffn_b[0]))

    sh1, sc1, g1, sh2, sc2, g2 = mods(1)
    n_q = w_out_c.shape[1]
    n_k = (w_in_c.shape[2] - n_q) // 2
    w_c = w_in_c[0]
    a1 = _swa(*_inproj1(x2, sc1, sh1, positions, w_c[:, :n_q].T.astype(BF16),
                        w_c[:, n_q:n_q + n_k].astype(BF16), w_c[:, n_q + n_k:].T.astype(BF16)),
              attn_sinks[0])
    x2 = _mixout(a1, None, x2, None, None, w_out_c[0].astype(BF16), g1,
                 vec(ln_mix_g[1]), vec(ln_mix_b[1]))
    x2 = _ffn(x2, sc2, sh2, wg, wu, wd, 1, g2, vec(ln_ffn_g[1]), vec(ln_ffn_b[1]))
    return x2.reshape(B, L, D)
```

```python
import functools
import math

import jax
import jax.numpy as jnp
from jax import lax
from jax.experimental import pallas as pl
from jax.experimental.pallas import tpu as pltpu

F32 = jnp.float32
BF16 = jnp.bfloat16

FOX_HEADS = 8
FOX_HEAD_DIM = 128
FOX_WIDTH = FOX_HEADS * FOX_HEAD_DIM
SSM_GROUP = 16
SSM_STATE = 64
SWA_HEAD_DIM = 64
SWA_Q_PER_KV = 8
WINDOW = 128
ROT_DIM = SWA_HEAD_DIM // 4
ROPE_THETA = 500000.0
DEPTH = 2
DEEPNORM_ALPHA = (2 * DEPTH) ** 0.25
LN_EPS = 1e-5

LANES = 128
SUBLANES = 8
NEG_BIG = -1e30

SSM_CHUNK = 16
SSM_BLOCK_GROUPS = LANES // SSM_GROUP


def _cparams(n_axes, vmem_mb):
    return pltpu.CompilerParams(
        dimension_semantics=("arbitrary",) * n_axes,
        vmem_limit_bytes=vmem_mb * 1024 * 1024)


def _sigmoid(x):
    return 1.0 / (1.0 + jnp.exp(-x))


def _layer_norm(r, g, b):
    mu = jnp.mean(r, axis=-1, keepdims=True)
    d = r - mu
    var = jnp.mean(d * d, axis=-1, keepdims=True)
    return d * lax.rsqrt(var + LN_EPS) * g + b


def _dot(a, b):
    return jnp.dot(a, b, preferred_element_type=F32)


def _dot_nt(a, b):
    return lax.dot_general(a, b, (((1,), (1,)), ((), ())), preferred_element_type=F32)


def _ada_kernel(c_ref, w_ref, b_ref, o_ref):
    c = c_ref[...]
    sc = c * _sigmoid(c)
    o_ref[0] = jnp.dot(sc, w_ref[0], precision=lax.Precision.HIGHEST,
                       preferred_element_type=F32) + b_ref[0]


def _ada(c, w_ada, b_ada):
    depth, d, n = w_ada.shape
    tn = 1536
    c8 = jnp.broadcast_to(c, (SUBLANES, d))
    out = pl.pallas_call(
        _ada_kernel,
        grid=(depth, n // tn),
        in_specs=[pl.BlockSpec((SUBLANES, d), lambda l, j: (0, 0)),
                  pl.BlockSpec((1, d, tn), lambda l, j: (l, 0, j)),
                  pl.BlockSpec((1, 1, tn), lambda l, j: (l, 0, j))],
        out_specs=pl.BlockSpec((1, SUBLANES, tn), lambda l, j: (l, 0, j)),
        out_shape=jax.ShapeDtypeStruct((depth, SUBLANES, n), F32),
        compiler_params=_cparams(2, 40),
        name="ada",
    )(c8, w_ada, b_ada.reshape(depth, 1, n))
    return out[:, 0:1, :]


N_SPLIT = 3
ONES_ROWS = 16
LOG2E = 1.0 / math.log(2.0)


def _inproj0_kernel(x_ref, sc_ref, sh_ref, w_ref, wvt_ref, wf_ref, bf_ref, pq_ref, pk_ref,
                    cq_ref, ck_ref, qa_ref, ka_ref, u_ref, vt_ref, carry_sc,
                    *, tm, scale):
    i = pl.program_id(0)
    hd = FOX_HEAD_DIM
    W = FOX_WIDTH

    def put_heads(dst_ref, val, off):
        for h in range(FOX_HEADS):
            dst_ref[:, 2 * h * hd + off:2 * h * hd + off + hd] = (
                val[:, h * hd:(h + 1) * hd].astype(BF16))

    hb = (x_ref[...] * (1.0 + sc_ref[...]) + sh_ref[...]).astype(BF16)
    z = _dot(hb, wf_ref[...]) + bf_ref[...]
    lf = jnp.minimum(z, 0.0) - jnp.log1p(jnp.exp(-jnp.abs(z)))

    def split(v):
        p1 = v.astype(BF16)
        r1 = v - p1.astype(F32)
        p2 = r1.astype(BF16)
        return p1, p2, (r1 - p2.astype(F32)).astype(BF16)

    row = lax.broadcasted_iota(jnp.int32, (tm, tm), 0)
    col = lax.broadcasted_iota(jnp.int32, (tm, tm), 1)
    tri = (col <= row).astype(BF16)
    cum = sum(_dot(tri, p) for p in split(lf))

    @pl.when(i == 0)
    def _():
        carry_sc[...] = jnp.zeros_like(carry_sc)

    f = cum + carry_sc[...]
    carry_sc[...] = f[tm - 1:tm, :]
    fp = jnp.concatenate(split(f * LOG2E), axis=1)
    put_heads(qa_ref, _dot(fp, pq_ref[...]) + cq_ref[...], hd)
    put_heads(ka_ref, _dot(fp, pk_ref[...]) + ck_ref[...], hd)
    put_heads(qa_ref, _dot(hb, w_ref[:, 0:W]) * scale, 0)
    put_heads(ka_ref, _dot(hb, w_ref[:, W:2 * W]), 0)
    u_ref[...] = _dot(hb, w_ref[:, 2 * W:3 * W]).astype(BF16)
    vt = _dot_nt(wvt_ref[...], hb)
    vt_ref[:, :, 0:hd, :] = vt.astype(BF16).reshape(FOX_HEADS, 1, hd, tm)
    vt_ref[:, :, hd:, :] = jnp.ones((FOX_HEADS, 1, ONES_ROWS, tm), BF16)


def _inproj0(x, sc, sh, w_qku, w_vt, wf, bf, tm):
    L, d = x.shape
    W = FOX_WIDTH
    hd = FOX_HEAD_DIM
    piece = jnp.arange(N_SPLIT)
    head = jnp.arange(FOX_HEADS)
    rows = (piece[:, None] * LANES + head[None, :]).reshape(-1)
    cols_k = (head[None, :] * hd + piece[:, None]).reshape(-1)
    cols_q = cols_k + N_SPLIT
    pq = jnp.zeros((N_SPLIT * LANES, W), F32).at[rows, cols_q].set(1.0).astype(BF16)
    pk = jnp.zeros((N_SPLIT * LANES, W), F32).at[rows, cols_k].set(-1.0).astype(BF16)
    lane = jnp.arange(W) % hd
    cq = (lane < N_SPLIT).astype(F32).reshape(1, W)
    ck = ((lane >= N_SPLIT) & (lane < 2 * N_SPLIT)).astype(F32).reshape(1, W)
    kern = functools.partial(_inproj0_kernel, tm=tm, scale=LOG2E / math.sqrt(hd))
    hv = hd + ONES_ROWS
    row = lambda i: (i, 0)

    def const(shape):
        return pl.BlockSpec(shape, lambda i: (0, 0), pipeline_mode=pl.Buffered(1))

    return pl.pallas_call(
        kern,
        grid=(L // tm,),
        in_specs=[pl.BlockSpec((tm, d), row),
                  const((1, d)), const((1, d)),
                  const((d, 3 * W)), const((W, d)),
                  const((d, LANES)), const((1, LANES)),
                  const((N_SPLIT * LANES, W)), const((N_SPLIT * LANES, W)),
                  const((1, W)), const((1, W))],
        out_specs=[pl.BlockSpec((tm, 2 * W), row),
                   pl.BlockSpec((tm, 2 * W), row),
                   pl.BlockSpec((tm, W), row),
                   pl.BlockSpec((FOX_HEADS, 1, hv, tm), lambda i: (0, i, 0, 0))],
        out_shape=[jax.ShapeDtypeStruct((L, 2 * W), BF16),
                   jax.ShapeDtypeStruct((L, 2 * W), BF16),
                   jax.ShapeDtypeStruct((L, W), BF16),
                   jax.ShapeDtypeStruct((FOX_HEADS, L // tm, hv, tm), BF16)],
        scratch_shapes=[pltpu.VMEM((1, LANES), F32)],
        compiler_params=_cparams(1, 48),
        name="inproj0",
    )(x, sc, sh, w_qku, w_vt, wf, bf, pq, pk, cq, ck)


M_INIT = -1e29


def _fox_kernel(qa_ref, ka_ref, vt_ref, o_ref, s0, s1, p0, p1, al0, al1, mb0, mb1, m_sc,
                acc_sc, mask_sc, *, tq, tk):
    first = jnp.logical_and(pl.program_id(0) == 0, pl.program_id(1) == 0)
    i = pl.program_id(1)
    last = 2 * i + 1

    @pl.when(first)
    def _():
        key = lax.broadcasted_iota(jnp.int32, (tk, tq), 0)
        qry = lax.broadcasted_iota(jnp.int32, (tk, tq), 1)
        mask_sc[0] = jnp.zeros((tk, tq), F32)
        mask_sc[1] = jnp.where(key <= qry, 0.0, NEG_BIG)
        mask_sc[2] = jnp.where(key + tk <= qry, 0.0, NEG_BIG)

    q = qa_ref[...]
    hd = o_ref.shape[1]
    s1[...] = jnp.full_like(s1, NEG_BIG)
    mb1[...] = jnp.full_like(mb1, NEG_BIG)
    p0[...] = jnp.zeros_like(p0)
    al0[...] = jnp.ones_like(al0)
    m_sc[...] = jnp.full_like(m_sc, M_INIT)
    acc_sc[...] = jnp.zeros_like(acc_sc)

    def scores(t, s_ref, mb_ref, masked):
        start = pl.multiple_of(t * tk, tk)
        s = _dot_nt(ka_ref[pl.ds(start, tk), :], q)
        if masked:
            s = s + mask_sc[jnp.clip(t - last + 2, 0, 2)]
        s_ref[...] = s
        mb_ref[...] = jnp.max(s, axis=0, keepdims=True)

    def softmax(s_ref, mb_ref, p_ref, al_ref):
        m_prev = m_sc[...]
        m_new = jnp.maximum(m_prev, mb_ref[...])
        m_sc[...] = m_new
        al_ref[...] = jnp.exp2(m_prev - m_new)
        p_ref[...] = jnp.exp2(s_ref[...] - m_new).astype(BF16)

    def values(t, p_ref, al_ref):
        blk = jnp.maximum(t, 0)
        acc_sc[...] = al_ref[...] * acc_sc[...] + _dot(vt_ref[0, blk], p_ref[...])

    def pair(u, carry, masked):
        t = 2 * u
        scores(t, s0, mb0, masked)
        softmax(s1, mb1, p1, al1)
        values(t - 2, p0, al0)
        scores(t + 1, s1, mb1, masked)
        softmax(s0, mb0, p0, al0)
        values(t - 1, p1, al1)
        return carry

    def quad(v, carry):
        pair(2 * v, carry, False)
        return pair(2 * v + 1, carry, False)

    n_quad = i // 2
    lax.fori_loop(0, n_quad, quad, 0)
    lax.fori_loop(2 * n_quad, i + 1, functools.partial(pair, masked=True), 0)
    softmax(s1, mb1, p1, al1)
    values(last - 1, p0, al0)
    values(last, p1, al1)

    acc = acc_sc[...]
    o_ref[...] = (acc[0:hd] / acc[hd:hd + 1]).T.astype(BF16)


def _fox(qa, ka, vt, tk):
    L = qa.shape[0]
    hd = FOX_HEAD_DIM
    hv = hd + ONES_ROWS
    tq = 2 * tk
    assert vt.shape == (FOX_HEADS, L // tk, hv, tk)
    row = pltpu.VMEM((1, tq), F32)
    return pl.pallas_call(
        functools.partial(_fox_kernel, tq=tq, tk=tk),
        grid=(FOX_HEADS, L // tq),
        in_specs=[pl.BlockSpec((tq, 2 * hd), lambda h, i: (i, h)),
                  pl.BlockSpec((L, 2 * hd), lambda h, i: (0, h)),
                  pl.BlockSpec((1, L // tk, hv, tk), lambda h, i: (h, 0, 0, 0))],
        out_specs=pl.BlockSpec((tq, hd), lambda h, i: (i, h)),
        out_shape=jax.ShapeDtypeStruct((L, FOX_WIDTH), BF16),
        scratch_shapes=[pltpu.VMEM((tk, tq), F32), pltpu.VMEM((tk, tq), F32),
                        pltpu.VMEM((tk, tq), BF16), pltpu.VMEM((tk, tq), BF16),
                        row, row, row, row, row,
                        pltpu.VMEM((hv, tq), F32), pltpu.VMEM((3, tk, tq), F32)],
        compiler_params=_cparams(2, 48),
        name="fox",
    )(qa, ka, vt)


def _taps_kernel(cre_ref, cim_ref, pre_ref, pim_ref, bre_ref, bim_ref, o_ref):
    T, H, P = SSM_CHUNK, SSM_GROUP, SSM_STATE
    nt = (((1,), (1,)), ((), ()))
    for g in range(SSM_BLOCK_GROUPS):
        cre = cre_ref[g][None]
        cim = cim_ref[g][None]
        pre = pre_ref[g][:, None, :]
        pim = pim_ref[g][:, None, :]
        wre = (cre * pre - cim * pim).reshape(T * H, P)
        wim = (cre * pim + cim * pre).reshape(T * H, P)
        k = (lax.dot_general(bre_ref[g], wre, nt, precision=lax.Precision.HIGHEST,
                             preferred_element_type=F32)
             - lax.dot_general(bim_ref[g], wim, nt, precision=lax.Precision.HIGHEST,
                               preferred_element_type=F32))
        o_ref[g] = k


def _taps(c_re, c_im, pw_re, pw_im, bbt_re, bbt_im):
    G = c_re.shape[0]
    gb = SSM_BLOCK_GROUPS
    T, H, P = SSM_CHUNK, SSM_GROUP, SSM_STATE
    return pl.pallas_call(
        _taps_kernel,
        grid=(G // gb,),
        in_specs=[pl.BlockSpec((gb, H, P), lambda b: (b, 0, 0)),
                  pl.BlockSpec((gb, H, P), lambda b: (b, 0, 0)),
                  pl.BlockSpec((gb, T, P), lambda b: (b, 0, 0)),
                  pl.BlockSpec((gb, T, P), lambda b: (b, 0, 0)),
                  pl.BlockSpec((gb, H, P), lambda b: (b, 0, 0)),
                  pl.BlockSpec((gb, H, P), lambda b: (b, 0, 0))],
        out_specs=pl.BlockSpec((gb, H, T * H), lambda b: (b, 0, 0)),
        out_shape=jax.ShapeDtypeStruct((G, H, T * H), F32),
        compiler_params=_cparams(1, 32),
        name="taps",
    )(c_re, c_im, pw_re, pw_im, bbt_re, bbt_im)


def _ssm_kernel(u_ref, base_ref, w2r_ref, w2i_ref, v3r_ref, v3i_ref, rs_ref, r3_ref,
                ar_ref, ai_ref, pr_ref, pi_ref, a2r_ref, a2i_ref, o_ref,
                ucat_sc, s_sc, x_sc, m1_sc, m2_sc, m3_sc, *, nrow):
    tp = pl.program_id(1)
    T = SSM_CHUNK
    half = x_sc.shape[1] // 2
    ngrp = nrow // SUBLANES
    lg_h = SSM_GROUP.bit_length() - 1
    lg_p = SSM_STATE.bit_length() - 1

    def same_group(shape, row_shift, col_and, col_shift):
        r = lax.shift_right_logical(lax.broadcasted_iota(jnp.int32, shape, 0), row_shift)
        c = lax.broadcasted_iota(jnp.int32, shape, 1)
        c = lax.shift_right_logical(jnp.bitwise_and(c, col_and), col_shift)
        return (r == c).astype(F32)

    @pl.when(tp == 0)
    def _():
        for s in range(T):
            ucat_sc[:, s * LANES:(s + 1) * LANES] = u_ref[s]
        mask3 = same_group((LANES, half), lg_h, half - 1, lg_p)
        for v_ref, lo in ((v3r_ref, 0), (v3i_ref, half)):
            e = _dot(v_ref[0].astype(BF16), r3_ref[...])
            for s in range(T):
                m3_sc[s * LANES:(s + 1) * LANES, lo:lo + half] = (
                    e[s * LANES:(s + 1) * LANES] * mask3).astype(BF16)
        s_sc[...] = _dot(ucat_sc[...], m3_sc[...])
        ar = ar_ref[0]
        ai = ai_ref[0]

        def step(r, carry):
            xr, xi = carry
            row = pl.multiple_of(r * SUBLANES, SUBLANES)
            x_sc[pl.ds(row, SUBLANES), 0:half] = xr
            x_sc[pl.ds(row, SUBLANES), half:2 * half] = xi
            sr = s_sc[pl.ds(row, SUBLANES), 0:half]
            si = s_sc[pl.ds(row, SUBLANES), half:2 * half]
            return ar * xr - ai * xi + sr, ar * xi + ai * xr + si

        zero = jnp.zeros((SUBLANES, half), F32)
        er, ei = lax.fori_loop(0, ngrp, step, (zero, zero))

        a2r = a2r_ref[0]
        a2i = a2i_ref[0]
        sub = lax.broadcasted_iota(jnp.int32, (SUBLANES, half), 0)
        zr = jnp.zeros((1, half), F32)
        zi = jnp.zeros((1, half), F32)
        zr_all = zero
        zi_all = zero
        for jj in range(1, SUBLANES):
            nzr = a2r * zr - a2i * zi + er[jj - 1:jj]
            nzi = a2r * zi + a2i * zr + ei[jj - 1:jj]
            zr, zi = nzr, nzi
            zr_all = jnp.where(sub == jj, zr, zr_all)
            zi_all = jnp.where(sub == jj, zi, zi_all)

        def fix(r, carry):
            row = pl.multiple_of(r * SUBLANES, SUBLANES)
            pr = pr_ref[0, pl.ds(r, 1), :]
            pi = pi_ref[0, pl.ds(r, 1), :]
            x_sc[pl.ds(row, SUBLANES), 0:half] = (
                x_sc[pl.ds(row, SUBLANES), 0:half] + pr * zr_all - pi * zi_all)
            x_sc[pl.ds(row, SUBLANES), half:2 * half] = (
                x_sc[pl.ds(row, SUBLANES), half:2 * half] + pr * zi_all + pi * zr_all)
            return carry

        lax.fori_loop(0, ngrp, fix, 0)

    base = base_ref[0].astype(BF16)
    mask1 = same_group((LANES, 2 * LANES), lg_h, LANES - 1, lg_h)
    for s in range(T):
        m1_sc[s * LANES:(s + 1) * LANES, :] = (_dot(base, rs_ref[s]) * mask1).astype(BF16)
    mask2 = same_group((half, 2 * LANES), lg_p, LANES - 1, lg_h)
    m2_sc[0:half, :] = (_dot(w2r_ref[0].astype(BF16), rs_ref[0]) * mask2).astype(BF16)
    m2_sc[half:2 * half, :] = (-_dot(w2i_ref[0].astype(BF16), rs_ref[0]) * mask2).astype(BF16)

    y = _dot(ucat_sc[...], m1_sc[...]) + _dot(x_sc[...].astype(BF16), m2_sc[...])
    o_ref[0] = y[:, 0:LANES].astype(BF16)
    o_ref[1] = y[:, LANES:2 * LANES].astype(BF16)


def _ssm(u_s, base, w2_re, w2_im, v3_re, v3_im, rs, r3, a_re, a_im, p_re, p_im, a2_re, a2_im):
    T, nrow, width = u_s.shape
    nb = width // LANES
    half = r3.shape[1]
    nstate = 2 * half
    ngrp = nrow // SUBLANES
    th = rs.shape[1]
    P = r3.shape[0]
    kern = functools.partial(_ssm_kernel, nrow=nrow)
    return pl.pallas_call(
        kern,
        grid=(nb, T // 2),
        in_specs=[pl.BlockSpec((T, nrow, LANES), lambda b, t: (0, 0, b)),
                  pl.BlockSpec((1, LANES, th), lambda b, t: (b, 0, 0)),
                  pl.BlockSpec((1, half, th), lambda b, t: (b, 0, 0)),
                  pl.BlockSpec((1, half, th), lambda b, t: (b, 0, 0)),
                  pl.BlockSpec((1, T * LANES, P), lambda b, t: (b, 0, 0)),
                  pl.BlockSpec((1, T * LANES, P), lambda b, t: (b, 0, 0)),
                  pl.BlockSpec((T, th, 2 * LANES), lambda b, t: (0, 0, t)),
                  pl.BlockSpec((P, half), lambda b, t: (0, 0)),
                  pl.BlockSpec((1, 1, half), lambda b, t: (b, 0, 0)),
                  pl.BlockSpec((1, 1, half), lambda b, t: (b, 0, 0)),
                  pl.BlockSpec((1, ngrp, half), lambda b, t: (b, 0, 0)),
                  pl.BlockSpec((1, ngrp, half), lambda b, t: (b, 0, 0)),
                  pl.BlockSpec((1, 1, half), lambda b, t: (b, 0, 0)),
                  pl.BlockSpec((1, 1, half), lambda b, t: (b, 0, 0))],
        out_specs=pl.BlockSpec((2, nrow, LANES), lambda b, t: (t, 0, b)),
        out_shape=jax.ShapeDtypeStruct((T, nrow, width), BF16),
        scratch_shapes=[pltpu.VMEM((nrow, T * LANES), BF16),
                        pltpu.VMEM((nrow, nstate), F32),
                        pltpu.VMEM((nrow, nstate), F32),
                        pltpu.VMEM((T * LANES, 2 * LANES), BF16),
                        pltpu.VMEM((nstate, 2 * LANES), BF16),
                        pltpu.VMEM((T * LANES, nstate), BF16)],
        compiler_params=_cparams(2, 48),
        name="ssm",
    )(u_s, base, w2_re, w2_im, v3_re, v3_im, rs, r3, a_re, a_im, p_re, p_im, a2_re, a2_im)


def _ssm_tables(lam_re, lam_im, log_dt, b_re, b_im, c_re, c_im, d_skip, ngrp):
    G, P = lam_re.shape
    H = SSM_GROUP
    T = SSM_CHUNK
    gb = SSM_BLOCK_GROUPS
    nb = G // gb
    dt = jnp.exp(log_dt)[:, None]
    mag = jnp.exp(lam_re * dt)
    lb_re = mag * jnp.cos(lam_im * dt)
    lb_im = mag * jnp.sin(lam_im * dt)
    den = lam_re * lam_re + lam_im * lam_im
    nr = lb_re - 1.0
    q_re = (nr * lam_re + lb_im * lam_im) / den
    q_im = (lb_im * lam_re - nr * lam_im) / den
    bb_re = q_re[..., None] * b_re - q_im[..., None] * b_im
    bb_im = q_re[..., None] * b_im + q_im[..., None] * b_re

    def powers(n):
        nf = n.astype(F32)[..., None, None]
        m = jnp.exp(nf * (lam_re * dt))
        return m * jnp.cos(nf * (lam_im * dt)), m * jnp.sin(nf * (lam_im * dt))

    pw_re, pw_im = powers(jnp.arange(T + 1))
    aT_re, aT_im = pw_re[T], pw_im[T]
    pT_re, pT_im = powers(T * jnp.arange(ngrp))
    a2_re, a2_im = powers(jnp.asarray(T * ngrp))

    bbt_re = jnp.transpose(bb_re, (0, 2, 1))
    bbt_im = jnp.transpose(bb_im, (0, 2, 1))
    taps = _taps(c_re, c_im,
                 jnp.transpose(pw_re[:T], (1, 0, 2)), jnp.transpose(pw_im[:T], (1, 0, 2)),
                 bbt_re, bbt_im)
    jj = jnp.arange(H)
    taps = taps.at[:, jj, jj].add(d_skip)
    base = taps.reshape(nb, gb * H, T * H)

    er = pw_re[:T][::-1][:, :, None, :]
    ei = pw_im[:T][::-1][:, :, None, :]
    v_re = er * bbt_re[None] - ei * bbt_im[None]
    v_im = er * bbt_im[None] + ei * bbt_re[None]

    def rows3(v):
        v = v.reshape(T, nb, gb * H, P)
        return jnp.transpose(v, (1, 0, 2, 3)).reshape(nb, T * LANES, P)

    fr = jnp.transpose(pw_re[1:T + 1], (1, 2, 0))[..., None]
    fi = jnp.transpose(pw_im[1:T + 1], (1, 2, 0))[..., None]
    ct_re = jnp.transpose(c_re, (0, 2, 1))[:, :, None, :]
    ct_im = jnp.transpose(c_im, (0, 2, 1))[:, :, None, :]
    w2_re = (ct_re * fr - ct_im * fi).reshape(nb, gb * P, T * H)
    w2_im = (ct_re * fi + ct_im * fr).reshape(nb, gb * P, T * H)

    rr = jnp.arange(T * H)
    cc = jnp.arange(T * LANES)
    same_i = (rr % H)[None, :, None] == (cc % H)[None, None, :]
    shifted = ((rr // H)[None, :, None] + jnp.arange(T)[:, None, None]
               == (cc // LANES)[None, None, :])
    rs = (same_i & shifted).astype(BF16)
    r3 = (jnp.arange(P)[:, None] == (jnp.arange(gb * P) % P)[None, :]).astype(BF16)

    def lanes(a):
        lead = a.shape[:-2]
        a = a.reshape(lead + (nb, gb * P))
        return jnp.moveaxis(a, -2, 0)

    a_re = lanes(aT_re)[:, None, :]
    a_im = lanes(aT_im)[:, None, :]
    p_re = lanes(pT_re)
    p_im = lanes(pT_im)
    a2_re = lanes(a2_re)[:, None, :]
    a2_im = lanes(a2_im)[:, None, :]
    return (base, w2_re, w2_im, rows3(v_re), rows3(v_im), rs, r3,
            a_re, a_im, p_re, p_im, a2_re, a2_im)


def _mixout_kernel(*refs, glu):
    if glu:
        (a_ref, y_ref, x_ref, wglu_ref, bglu_ref, wo_ref, g_ref, lng_ref, lnb_ref,
         o_ref) = refs
        y = y_ref[...].astype(F32)
        s = y * (0.5 * (1.0 + jnp.tanh(math.sqrt(2.0 / math.pi)
                                       * (y + 0.044715 * (y * y * y)))))
        z = _dot(s.astype(BF16), wglu_ref[...]) + bglu_ref[...]
        s2 = (s * _sigmoid(z)).astype(BF16)
        wa = a_ref.shape[1]
        mix = _dot(a_ref[...], wo_ref[0:wa, :]) + _dot(s2, wo_ref[wa:, :])
    else:
        a_ref, x_ref, wo_ref, g_ref, lng_ref, lnb_ref, o_ref = refs
        mix = _dot(a_ref[...], wo_ref[...])
    r = DEEPNORM_ALPHA * x_ref[...] + g_ref[...] * mix
    o_ref[...] = _layer_norm(r, lng_ref[...], lnb_ref[...])


def _mixout(a, y, x, wglu, bglu, wo, g, lng, lnb):
    L, d = x.shape
    tm = 512
    glu = y is not None
    row = lambda i: (i, 0)
    fix = lambda i: (0, 0)
    vec = pl.BlockSpec((1, d), fix)
    if glu:
        ws = y.shape[1]
        args = (a, y, x, wglu, bglu, wo, g, lng, lnb)
        in_specs = [pl.BlockSpec((tm, a.shape[1]), row), pl.BlockSpec((tm, ws), row),
                    pl.BlockSpec((tm, d), row), pl.BlockSpec((ws, ws), fix),
                    pl.BlockSpec((1, ws), fix), pl.BlockSpec(wo.shape, fix), vec, vec, vec]
    else:
        args = (a, x, wo, g, lng, lnb)
        in_specs = [pl.BlockSpec((tm, a.shape[1]), row), pl.BlockSpec((tm, d), row),
                    pl.BlockSpec(wo.shape, fix), vec, vec, vec]
    return pl.pallas_call(
        functools.partial(_mixout_kernel, glu=glu),
        grid=(L // tm,),
        in_specs=in_specs,
        out_specs=pl.BlockSpec((tm, d), row),
        out_shape=jax.ShapeDtypeStruct((L, d), F32),
        compiler_params=_cparams(1, 48),
        name="mixout_glu" if glu else "mixout",
    )(*args)


def _ffn_kernel(x_ref, sc_ref, sh_ref, wg_ref, wu_ref, wd_ref, g_ref, lng_ref, lnb_ref,
                o_ref, h_sc):
    j = pl.program_id(1)

    @pl.when(j == 0)
    def _():
        h_sc[...] = (x_ref[...] * (1.0 + sc_ref[...]) + sh_ref[...]).astype(BF16)
        o_ref[...] = jnp.zeros_like(o_ref)

    h = h_sc[...]
    gt = _dot(h, wg_ref[0])
    ut = _dot(h, wu_ref[0])
    act = (gt * _sigmoid(gt) * ut).astype(BF16)
    o_ref[...] += _dot(act, wd_ref[0])

    @pl.when(j == pl.num_programs(1) - 1)
    def _():
        r = DEEPNORM_ALPHA * x_ref[...] + g_ref[...] * o_ref[...]
        o_ref[...] = _layer_norm(r, lng_ref[...], lnb_ref[...])


def _ffn(x, sc, sh, wg, wu, wd, layer, g, lng, lnb):
    L, d = x.shape
    dff = wg.shape[2]
    tm, tf = 1024, 256
    row = lambda i, j: (i, 0)
    vec = pl.BlockSpec((1, d), lambda i, j: (0, 0))
    return pl.pallas_call(
        _ffn_kernel,
        grid=(L // tm, dff // tf),
        in_specs=[pl.BlockSpec((tm, d), row), vec, vec,
                  pl.BlockSpec((1, d, tf), lambda i, j: (layer, 0, j)),
                  pl.BlockSpec((1, d, tf), lambda i, j: (layer, 0, j)),
                  pl.BlockSpec((1, tf, d), lambda i, j: (layer, j, 0)),
                  vec, vec, vec],
        out_specs=pl.BlockSpec((tm, d), row),
        out_shape=jax.ShapeDtypeStruct((L, d), F32),
        scratch_shapes=[pltpu.VMEM((tm, d), BF16)],
        compiler_params=_cparams(2, 58),
        name="ffn",
    )(x, sc, sh, wg, wu, wd, g, lng, lnb)


def _inproj1_kernel(x_ref, sc_ref, sh_ref, posc_ref, posr_ref, freq_ref, slo_ref, shi_ref,
                    fcol_ref, wqt_ref, wk_ref, wvt_ref, qt_ref, k_ref, vt_ref,
                    *, slab, scale):
    half = ROT_DIM // 2
    dh = SWA_HEAD_DIM
    hb = (x_ref[...] * (1.0 + sc_ref[...]) + sh_ref[...]).astype(BF16)

    ang = fcol_ref[...] * posr_ref[...]
    cs = jnp.cos(ang)
    sn = jnp.sin(ang)
    for c in range(qt_ref.shape[0] // slab):
        r = _dot_nt(wqt_ref[c * slab:(c + 1) * slab, :], hb)
        for h in range(slab // dh):
            b = h * dh
            x1 = r[b:b + half]
            x2 = r[b + half:b + ROT_DIM]
            rot = jnp.concatenate([x1 * cs - x2 * sn, x2 * cs + x1 * sn], axis=0)
            qt_ref[c * slab + b:c * slab + b + ROT_DIM, :] = (rot * scale).astype(BF16)
            qt_ref[c * slab + b + ROT_DIM:c * slab + b + dh, :] = (
                r[b + ROT_DIM:b + dh] * scale).astype(BF16)

    kk = _dot(hb, wk_ref[...])
    ang = posc_ref[...] * freq_ref[...]
    cs = jnp.cos(ang)
    sn = jnp.sin(ang)
    sin_lo = sn * slo_ref[...]
    sin_hi = sn * shi_ref[...]
    for c in range(kk.shape[1] // LANES):
        xc = kk[:, c * LANES:(c + 1) * LANES]
        k_ref[:, c * LANES:(c + 1) * LANES] = (
            xc * cs + pltpu.roll(xc, LANES - half, 1) * sin_lo
            + pltpu.roll(xc, half, 1) * sin_hi).astype(BF16)
    vt_ref[...] = _dot_nt(wvt_ref[...], hb).astype(BF16)


def _inproj1(x, sc, sh, pos, w_qt, w_k, w_vt):
    L, d = x.shape
    n_q = w_qt.shape[0]
    n_k = w_k.shape[1]
    tm = 512
    half = ROT_DIM // 2
    inv_freq = jnp.power(jnp.float32(ROPE_THETA), -jnp.arange(half, dtype=F32) * (2.0 / ROT_DIM))
    dlane = jnp.arange(LANES) % SWA_HEAD_DIM
    freq = jnp.where(dlane < ROT_DIM, inv_freq[dlane % half], 0.0).reshape(1, LANES)
    slo = jnp.where(dlane < half, -1.0, 0.0).astype(F32).reshape(1, LANES)
    shi = jnp.where((dlane >= half) & (dlane < ROT_DIM), 1.0, 0.0).astype(F32).reshape(1, LANES)
    posf = pos.astype(F32)
    kern = functools.partial(_inproj1_kernel, slab=512, scale=LOG2E / math.sqrt(SWA_HEAD_DIM))

    def const(shape):
        return pl.BlockSpec(shape, lambda i: (0, 0), pipeline_mode=pl.Buffered(1))

    return pl.pallas_call(
        kern,
        grid=(L // tm,),
        in_specs=[pl.BlockSpec((tm, d), lambda i: (i, 0)),
                  const((1, d)), const((1, d)),
                  pl.BlockSpec((tm, 1), lambda i: (i, 0)),
                  pl.BlockSpec((1, tm), lambda i: (0, i)),
                  const((1, LANES)), const((1, LANES)), const((1, LANES)), const((half, 1)),
                  const((n_q, d)), const((d, n_k)), const((n_k, d))],
        out_specs=[pl.BlockSpec((n_q, tm), lambda i: (0, i)),
                   pl.BlockSpec((tm, n_k), lambda i: (i, 0)),
                   pl.BlockSpec((n_k, tm), lambda i: (0, i))],
        out_shape=[jax.ShapeDtypeStruct((n_q, L), BF16),
                   jax.ShapeDtypeStruct((L, n_k), BF16),
                   jax.ShapeDtypeStruct((n_k, L), BF16)],
        compiler_params=_cparams(1, 40),
        name="inproj1",
    )(x, sc, sh, posf.reshape(L, 1), posf.reshape(1, L), freq, slo, shi,
      inv_freq.reshape(half, 1), w_qt, w_k, w_vt)


SWA_BLOCKS_PER_STEP = 2


def _swa_kernel(sink_ref, qt_ref, kc_ref, kp_ref, vtc_ref, vtp_ref, o_ref, *, n_kv):
    n = pl.program_id(0)
    W = WINDOW
    dh = SWA_HEAD_DIM
    G = SWA_Q_PER_KV
    kj = lax.broadcasted_iota(jnp.int32, (2 * W, W), 0)
    qi = lax.broadcasted_iota(jnp.int32, (2 * W, W), 1)
    for qb in range(SWA_BLOCKS_PER_STEP):
        cur = slice(qb * W, (qb + 1) * W)
        prev = slice((qb - 1) * W, qb * W)
        lo_key = jnp.where(n > 0, qi, W - 1) if qb == 0 else qi
        band = jnp.logical_and(kj > lo_key, kj <= qi + W)
        mask1 = jnp.where(band, 0.0, NEG_BIG)
        mask = jnp.concatenate([mask1] * G, axis=1)
        for kvh in range(n_kv):
            ksl = slice(kvh * dh, (kvh + 1) * dh)
            k_prev = kp_ref[:, ksl] if qb == 0 else kc_ref[prev, ksl]
            vt_prev = vtp_ref[ksl, :] if qb == 0 else vtc_ref[ksl, prev]
            kk = jnp.concatenate([k_prev, kc_ref[cur, ksl]], axis=0)
            vv = jnp.concatenate([vt_prev, vtc_ref[ksl, cur]], axis=1)
            q = jnp.concatenate([qt_ref[(kvh * G + g) * dh:(kvh * G + g + 1) * dh, cur]
                                 for g in range(G)], axis=1)
            s = _dot(kk, q) + mask
            sink = jnp.concatenate([jnp.full((1, W), sink_ref[kvh * G + g] * LOG2E, F32)
                                    for g in range(G)], axis=1)
            m = jnp.maximum(jnp.max(s, axis=0, keepdims=True), sink)
            p = jnp.exp2(s - m)
            den = jnp.sum(p, axis=0, keepdims=True) + jnp.exp2(sink - m)
            o = _dot(vv, p.astype(BF16)) / den
            for gp in range(G // 2):
                two = jnp.concatenate([o[:, (2 * gp) * W:(2 * gp + 1) * W],
                                       o[:, (2 * gp + 1) * W:(2 * gp + 2) * W]], axis=0)
                lo = (kvh * G + 2 * gp) * dh
                o_ref[cur, lo:lo + 2 * dh] = two.T.astype(BF16)


def _swa(qt, k, vt, sinks):
    n_q, L = qt.shape
    kw = k.shape[1]
    n_kv = kw // SWA_HEAD_DIM
    W = WINDOW
    nb = SWA_BLOCKS_PER_STEP
    S = nb * W
    prev_blk = lambda n: jnp.maximum(nb * n - 1, 0)
    grid_spec = pltpu.PrefetchScalarGridSpec(
        num_scalar_prefetch=1,
        grid=(L // S,),
        in_specs=[pl.BlockSpec((n_q, S), lambda n, s: (0, n)),
                  pl.BlockSpec((S, kw), lambda n, s: (n, 0)),
                  pl.BlockSpec((W, kw), lambda n, s: (prev_blk(n), 0)),
                  pl.BlockSpec((kw, S), lambda n, s: (0, n)),
                  pl.BlockSpec((kw, W), lambda n, s: (0, prev_blk(n)))],
        out_specs=pl.BlockSpec((S, n_q), lambda n, s: (n, 0)),
    )
    return pl.pallas_call(
        functools.partial(_swa_kernel, n_kv=n_kv),
        grid_spec=grid_spec,
        out_shape=jax.ShapeDtypeStruct((L, n_q), BF16),
        compiler_params=_cparams(1, 32),
        name="swa",
    )(sinks, qt, k, k, vt, vt)


def kernel(x, c, positions, w_in_ab, b_forget, ssm_lambda_re, ssm_lambda_im, ssm_log_dt, ssm_b_re, ssm_b_im, ssm_c_re, ssm_c_im, ssm_d, w_glu, b_glu, w_out_ab, w_in_c, attn_sinks, w_out_c, w_ada, b_ada, ln_mix_g, ln_mix_b, ln_ffn_g, ln_ffn_b, w_ffn_gate, w_ffn_up, w_ffn_down):
    B, L, D = x.shape
    assert B == 1
    x2 = x.reshape(L, D)
    mod = _ada(c, w_ada, b_ada)

    def mods(layer):
        return [mod[layer, :, k * D:(k + 1) * D] for k in range(6)]

    def vec(a):
        return a.reshape(1, -1)

    sh1, sc1, g1, sh2, sc2, g2 = mods(0)
    W = FOX_WIDTH
    w_in = w_in_ab[0]
    w_qku = jnp.concatenate([w_in[:, :2 * W], w_in[:, 3 * W + FOX_HEADS:]], 1).astype(BF16)
    w_vt = w_in[:, 2 * W:3 * W].T.astype(BF16)
    w_f = jnp.pad(w_in[:, 3 * W:3 * W + FOX_HEADS], ((0, 0), (0, LANES - FOX_HEADS))).astype(BF16)
    b_f = jnp.pad(b_forget[0], (0, LANES - FOX_HEADS)).reshape(1, LANES)
    tq = 512
    qa, ka, u, vt = _inproj0(x2, sc1, sh1, w_qku, w_vt, w_f, b_f, tq)
    a_out = _fox(qa, ka, vt, tq)

    T = SSM_CHUNK
    nrow = L // T
    nsub = nrow // SUBLANES
    width = u.shape[1]
    u_s = u.reshape(SUBLANES, nsub, T, width).transpose(2, 1, 0, 3).reshape(T, nrow, width)
    tables = _ssm_tables(ssm_lambda_re[0], ssm_lambda_im[0], ssm_log_dt[0], ssm_b_re[0],
                         ssm_b_im[0], ssm_c_re[0], ssm_c_im[0], ssm_d[0], nsub)
    y_s = _ssm(u_s, *tables)
    y = y_s.reshape(T, nsub, SUBLANES, width).transpose(2, 1, 0, 3).reshape(L, width)

    x2 = _mixout(a_out, y, x2, w_glu[0].astype(BF16), vec(b_glu[0]), w_out_ab[0].astype(BF16),
                 g1, vec(ln_mix_g[0]), vec(ln_mix_b[0]))
    wg, wu, wd = (w.astype(BF16) for w in (w_ffn_gate, w_ffn_up, w_ffn_down))
    x2 = _ffn(x2, sc2, sh2, wg, wu, wd, 0, g2, vec(ln_ffn_g[0]), vec(ln_ffn_b[0]))

    sh1, sc1, g1, sh2, sc2, g2 = mods(1)
    n_q = w_out_c.shape[1]
    n_k = (w_in_c.shape[2] - n_q) // 2
    w_c = w_in_c[0]
    a1 = _swa(*_inproj1(x2, sc1, sh1, positions, w_c[:, :n_q].T.astype(BF16),
                        w_c[:, n_q:n_q + n_k].astype(BF16), w_c[:, n_q + n_k:].T.astype(BF16)),
              attn_sinks[0])
    x2 = _mixout(a1, None, x2, None, None, w_out_c[0].astype(BF16), g1,
                 vec(ln_mix_g[1]), vec(ln_mix_b[1]))
    x2 = _ffn(x2, sc2, sh2, wg, wu, wd, 1, g2, vec(ln_ffn_g[1]), vec(ln_ffn_b[1]))
    return x2.reshape(B, L, D)
```

```python
import functools
import math

import jax
import jax.numpy as jnp
from jax import lax
from jax.experimental import pallas as pl
from jax.experimental.pallas import tpu as pltpu

F32 = jnp.float32
BF16 = jnp.bfloat16

FOX_HEADS = 8
FOX_HEAD_DIM = 128
FOX_WIDTH = FOX_HEADS * FOX_HEAD_DIM
SSM_GROUP = 16
SSM_STATE = 64
SWA_HEAD_DIM = 64
SWA_Q_PER_KV = 8
WINDOW = 128
ROT_DIM = SWA_HEAD_DIM // 4
ROPE_THETA = 500000.0
DEPTH = 2
DEEPNORM_ALPHA = (2 * DEPTH) ** 0.25
LN_EPS = 1e-5

LANES = 128
SUBLANES = 8
NEG_BIG = -1e30

SSM_CHUNK = 16
SSM_BLOCK_GROUPS = LANES // SSM_GROUP


def _cparams(n_axes, vmem_mb):
    return pltpu.CompilerParams(
        dimension_semantics=("arbitrary",) * n_axes,
        vmem_limit_bytes=vmem_mb * 1024 * 1024)


def _sigmoid(x):
    return 1.0 / (1.0 + jnp.exp(-x))


def _layer_norm(r, g, b):
    mu = jnp.mean(r, axis=-1, keepdims=True)
    d = r - mu
    var = jnp.mean(d * d, axis=-1, keepdims=True)
    return d * lax.rsqrt(var + LN_EPS) * g + b


def _dot(a, b):
    return jnp.dot(a, b, preferred_element_type=F32)


def _dot_nt(a, b):
    return lax.dot_general(a, b, (((1,), (1,)), ((), ())), preferred_element_type=F32)


def _ada_kernel(c_ref, w_ref, b_ref, o_ref):
    d, tn = w_ref.shape[1], w_ref.shape[2]
    c = c_ref[...]
    sc = c * _sigmoid(c)
    sc = jnp.concatenate([sc] * (tn // LANES), axis=1)
    part = jnp.sum((w_ref[0] * sc).reshape(d // SUBLANES, SUBLANES, tn), axis=0)
    row = jnp.sum(part, axis=0, keepdims=True) + b_ref[0]
    o_ref[0] = jnp.broadcast_to(row, (SUBLANES, tn))


def _ada(c, w_ada, b_ada):
    depth, d, n = w_ada.shape
    tn = 1536
    c_lanes = jnp.broadcast_to(c.reshape(d, 1), (d, LANES))
    out = pl.pallas_call(
        _ada_kernel,
        grid=(depth, n // tn),
        in_specs=[pl.BlockSpec((d, LANES), lambda l, j: (0, 0)),
                  pl.BlockSpec((1, d, tn), lambda l, j: (l, 0, j)),
                  pl.BlockSpec((1, 1, tn), lambda l, j: (l, 0, j))],
        out_specs=pl.BlockSpec((1, SUBLANES, tn), lambda l, j: (l, 0, j)),
        out_shape=jax.ShapeDtypeStruct((depth, SUBLANES, n), F32),
        compiler_params=_cparams(2, 40),
        name="ada",
    )(c_lanes, w_ada, b_ada.reshape(depth, 1, n))
    return out[:, 0:1, :]


N_SPLIT = 3
ONES_ROWS = 16
LOG2E = 1.0 / math.log(2.0)


def _inproj0_kernel(x_ref, sc_ref, sh_ref, w_ref, wvt_ref, wf_ref, bf_ref, pq_ref, pk_ref,
                    cq_ref, ck_ref, qa_ref, ka_ref, u_ref, vt_ref, carry_sc,
                    *, tm, scale):
    i = pl.program_id(0)
    hd = FOX_HEAD_DIM
    W = FOX_WIDTH

    def put_heads(dst_ref, val, off):
        for h in range(FOX_HEADS):
            dst_ref[:, 2 * h * hd + off:2 * h * hd + off + hd] = (
                val[:, h * hd:(h + 1) * hd].astype(BF16))

    hb = (x_ref[...] * (1.0 + sc_ref[...]) + sh_ref[...]).astype(BF16)
    z = _dot(hb, wf_ref[...]) + bf_ref[...]
    lf = jnp.minimum(z, 0.0) - jnp.log1p(jnp.exp(-jnp.abs(z)))

    def split(v):
        p1 = v.astype(BF16)
        r1 = v - p1.astype(F32)
        p2 = r1.astype(BF16)
        return p1, p2, (r1 - p2.astype(F32)).astype(BF16)

    row = lax.broadcasted_iota(jnp.int32, (tm, tm), 0)
    col = lax.broadcasted_iota(jnp.int32, (tm, tm), 1)
    tri = (col <= row).astype(BF16)
    cum = sum(_dot(tri, p) for p in split(lf))

    @pl.when(i == 0)
    def _():
        carry_sc[...] = jnp.zeros_like(carry_sc)

    f = cum + carry_sc[...]
    carry_sc[...] = f[tm - 1:tm, :]
    fp = jnp.concatenate(split(f * LOG2E), axis=1)
    put_heads(qa_ref, _dot(fp, pq_ref[...]) + cq_ref[...], hd)
    put_heads(ka_ref, _dot(fp, pk_ref[...]) + ck_ref[...], hd)
    put_heads(qa_ref, _dot(hb, w_ref[:, 0:W]) * scale, 0)
    put_heads(ka_ref, _dot(hb, w_ref[:, W:2 * W]), 0)
    u_ref[...] = _dot(hb, w_ref[:, 2 * W:3 * W]).astype(BF16)
    vt = _dot_nt(wvt_ref[...], hb)
    vt_ref[:, :, 0:hd, :] = vt.astype(BF16).reshape(FOX_HEADS, 1, hd, tm)
    vt_ref[:, :, hd:, :] = jnp.ones((FOX_HEADS, 1, ONES_ROWS, tm), BF16)


def _inproj0(x, sc, sh, w_qku, w_vt, wf, bf, tm):
    L, d = x.shape
    W = FOX_WIDTH
    hd = FOX_HEAD_DIM
    piece = jnp.arange(N_SPLIT)
    head = jnp.arange(FOX_HEADS)
    rows = (piece[:, None] * LANES + head[None, :]).reshape(-1)
    cols_k = (head[None, :] * hd + piece[:, None]).reshape(-1)
    cols_q = cols_k + N_SPLIT
    pq = jnp.zeros((N_SPLIT * LANES, W), F32).at[rows, cols_q].set(1.0).astype(BF16)
    pk = jnp.zeros((N_SPLIT * LANES, W), F32).at[rows, cols_k].set(-1.0).astype(BF16)
    lane = jnp.arange(W) % hd
    cq = (lane < N_SPLIT).astype(F32).reshape(1, W)
    ck = ((lane >= N_SPLIT) & (lane < 2 * N_SPLIT)).astype(F32).reshape(1, W)
    kern = functools.partial(_inproj0_kernel, tm=tm, scale=LOG2E / math.sqrt(hd))
    hv = hd + ONES_ROWS
    row = lambda i: (i, 0)

    def const(shape):
        return pl.BlockSpec(shape, lambda i: (0, 0), pipeline_mode=pl.Buffered(1))

    return pl.pallas_call(
        kern,
        grid=(L // tm,),
        in_specs=[pl.BlockSpec((tm, d), row),
                  const((1, d)), const((1, d)),
                  const((d, 3 * W)), const((W, d)),
                  const((d, LANES)), const((1, LANES)),
                  const((N_SPLIT * LANES, W)), const((N_SPLIT * LANES, W)),
                  const((1, W)), const((1, W))],
        out_specs=[pl.BlockSpec((tm, 2 * W), row),
                   pl.BlockSpec((tm, 2 * W), row),
                   pl.BlockSpec((tm, W), row),
                   pl.BlockSpec((FOX_HEADS, 1, hv, tm), lambda i: (0, i, 0, 0))],
        out_shape=[jax.ShapeDtypeStruct((L, 2 * W), BF16),
                   jax.ShapeDtypeStruct((L, 2 * W), BF16),
                   jax.ShapeDtypeStruct((L, W), BF16),
                   jax.ShapeDtypeStruct((FOX_HEADS, L // tm, hv, tm), BF16)],
        scratch_shapes=[pltpu.VMEM((1, LANES), F32)],
        compiler_params=_cparams(1, 48),
        name="inproj0",
    )(x, sc, sh, w_qku, w_vt, wf, bf, pq, pk, cq, ck)


M_INIT = -1e29


def _fox_kernel(qa_ref, ka_ref, vt_ref, o_ref, s0, s1, p0, p1, al0, al1, mb0, mb1, m_sc,
                acc_sc, mask_sc, *, tq, tk):
    first = jnp.logical_and(pl.program_id(0) == 0, pl.program_id(1) == 0)
    i = pl.program_id(1)
    last = 2 * i + 1

    @pl.when(first)
    def _():
        key = lax.broadcasted_iota(jnp.int32, (tk, tq), 0)
        qry = lax.broadcasted_iota(jnp.int32, (tk, tq), 1)
        mask_sc[0] = jnp.zeros((tk, tq), F32)
        mask_sc[1] = jnp.where(key <= qry, 0.0, NEG_BIG)
        mask_sc[2] = jnp.where(key + tk <= qry, 0.0, NEG_BIG)

    q = qa_ref[...]
    hd = o_ref.shape[1]
    s1[...] = jnp.full_like(s1, NEG_BIG)
    mb1[...] = jnp.full_like(mb1, NEG_BIG)
    p0[...] = jnp.zeros_like(p0)
    al0[...] = jnp.ones_like(al0)
    m_sc[...] = jnp.full_like(m_sc, M_INIT)
    acc_sc[...] = jnp.zeros_like(acc_sc)

    def scores(t, s_ref, mb_ref, masked):
        start = pl.multiple_of(t * tk, tk)
        s = _dot_nt(ka_ref[pl.ds(start, tk), :], q)
        if masked:
            s = s + mask_sc[jnp.clip(t - last + 2, 0, 2)]
        s_ref[...] = s
        mb_ref[...] = jnp.max(s, axis=0, keepdims=True)

    def softmax(s_ref, mb_ref, p_ref, al_ref):
        m_prev = m_sc[...]
        m_new = jnp.maximum(m_prev, mb_ref[...])
        m_sc[...] = m_new
        al_ref[...] = jnp.exp2(m_prev - m_new)
        p_ref[...] = jnp.exp2(s_ref[...] - m_new).astype(BF16)

    def values(t, p_ref, al_ref):
        blk = jnp.maximum(t, 0)
        acc_sc[...] = al_ref[...] * acc_sc[...] + _dot(vt_ref[0, blk], p_ref[...])

    def pair(u, carry, masked):
        t = 2 * u
        scores(t, s0, mb0, masked)
        softmax(s1, mb1, p1, al1)
        values(t - 2, p0, al0)
        scores(t + 1, s1, mb1, masked)
        softmax(s0, mb0, p0, al0)
        values(t - 1, p1, al1)
        return carry

    def quad(v, carry):
        pair(2 * v, carry, False)
        return pair(2 * v + 1, carry, False)

    n_quad = i // 2
    lax.fori_loop(0, n_quad, quad, 0)
    lax.fori_loop(2 * n_quad, i + 1, functools.partial(pair, masked=True), 0)
    softmax(s1, mb1, p1, al1)
    values(last - 1, p0, al0)
    values(last, p1, al1)

    acc = acc_sc[...]
    o_ref[...] = (acc[0:hd] / acc[hd:hd + 1]).T.astype(BF16)


def _fox(qa, ka, vt, tk):
    L = qa.shape[0]
    hd = FOX_HEAD_DIM
    hv = hd + ONES_ROWS
    tq = 2 * tk
    assert vt.shape == (FOX_HEADS, L // tk, hv, tk)
    row = pltpu.VMEM((1, tq), F32)
    return pl.pallas_call(
        functools.partial(_fox_kernel, tq=tq, tk=tk),
        grid=(FOX_HEADS, L // tq),
        in_specs=[pl.BlockSpec((tq, 2 * hd), lambda h, i: (i, h)),
                  pl.BlockSpec((L, 2 * hd), lambda h, i: (0, h)),
                  pl.BlockSpec((1, L // tk, hv, tk), lambda h, i: (h, 0, 0, 0))],
        out_specs=pl.BlockSpec((tq, hd), lambda h, i: (i, h)),
        out_shape=jax.ShapeDtypeStruct((L, FOX_WIDTH), BF16),
        scratch_shapes=[pltpu.VMEM((tk, tq), F32), pltpu.VMEM((tk, tq), F32),
                        pltpu.VMEM((tk, tq), BF16), pltpu.VMEM((tk, tq), BF16),
                        row, row, row, row, row,
                        pltpu.VMEM((hv, tq), F32), pltpu.VMEM((3, tk, tq), F32)],
        compiler_params=_cparams(2, 48),
        name="fox",
    )(qa, ka, vt)


def _taps_kernel(cre_ref, cim_ref, pre_ref, pim_ref, bre_ref, bim_ref, o_ref):
    T, H, P = SSM_CHUNK, SSM_GROUP, SSM_STATE
    nt = (((1,), (1,)), ((), ()))
    for g in range(SSM_BLOCK_GROUPS):
        cre = cre_ref[g][None]
        cim = cim_ref[g][None]
        pre = pre_ref[g][:, None, :]
        pim = pim_ref[g][:, None, :]
        wre = (cre * pre - cim * pim).reshape(T * H, P)
        wim = (cre * pim + cim * pre).reshape(T * H, P)
        k = (lax.dot_general(bre_ref[g], wre, nt, precision=lax.Precision.HIGHEST,
                             preferred_element_type=F32)
             - lax.dot_general(bim_ref[g], wim, nt, precision=lax.Precision.HIGHEST,
                               preferred_element_type=F32))
        o_ref[g] = k


def _taps(c_re, c_im, pw_re, pw_im, bbt_re, bbt_im):
    G = c_re.shape[0]
    gb = SSM_BLOCK_GROUPS
    T, H, P = SSM_CHUNK, SSM_GROUP, SSM_STATE
    return pl.pallas_call(
        _taps_kernel,
        grid=(G // gb,),
        in_specs=[pl.BlockSpec((gb, H, P), lambda b: (b, 0, 0)),
                  pl.BlockSpec((gb, H, P), lambda b: (b, 0, 0)),
                  pl.BlockSpec((gb, T, P), lambda b: (b, 0, 0)),
                  pl.BlockSpec((gb, T, P), lambda b: (b, 0, 0)),
                  pl.BlockSpec((gb, H, P), lambda b: (b, 0, 0)),
                  pl.BlockSpec((gb, H, P), lambda b: (b, 0, 0))],
        out_specs=pl.BlockSpec((gb, H, T * H), lambda b: (b, 0, 0)),
        out_shape=jax.ShapeDtypeStruct((G, H, T * H), F32),
        compiler_params=_cparams(1, 32),
        name="taps",
    )(c_re, c_im, pw_re, pw_im, bbt_re, bbt_im)


def _ssm_kernel(u_ref, base_ref, w2r_ref, w2i_ref, v3r_ref, v3i_ref, rs_ref, r3_ref,
                ar_ref, ai_ref, pr_ref, pi_ref, a2r_ref, a2i_ref, o_ref,
                ucat_sc, s_sc, x_sc, m1_sc, m2_sc, m3_sc, *, nrow):
    tp = pl.program_id(1)
    T = SSM_CHUNK
    half = x_sc.shape[1] // 2
    ngrp = nrow // SUBLANES
    lg_h = SSM_GROUP.bit_length() - 1
    lg_p = SSM_STATE.bit_length() - 1

    def same_group(shape, row_shift, col_and, col_shift):
        r = lax.shift_right_logical(lax.broadcasted_iota(jnp.int32, shape, 0), row_shift)
        c = lax.broadcasted_iota(jnp.int32, shape, 1)
        c = lax.shift_right_logical(jnp.bitwise_and(c, col_and), col_shift)
        return (r == c).astype(F32)

    @pl.when(tp == 0)
    def _():
        for s in range(T):
            ucat_sc[:, s * LANES:(s + 1) * LANES] = u_ref[s]
        mask3 = same_group((LANES, half), lg_h, half - 1, lg_p)
        for v_ref, lo in ((v3r_ref, 0), (v3i_ref, half)):
            e = _dot(v_ref[0].astype(BF16), r3_ref[...])
            for s in range(T):
                m3_sc[s * LANES:(s + 1) * LANES, lo:lo + half] = (
                    e[s * LANES:(s + 1) * LANES] * mask3).astype(BF16)
        s_sc[...] = _dot(ucat_sc[...], m3_sc[...])
        ar = ar_ref[0]
        ai = ai_ref[0]

        def step(r, carry):
            xr, xi = carry
            row = pl.multiple_of(r * SUBLANES, SUBLANES)
            x_sc[pl.ds(row, SUBLANES), 0:half] = xr
            x_sc[pl.ds(row, SUBLANES), half:2 * half] = xi
            sr = s_sc[pl.ds(row, SUBLANES), 0:half]
            si = s_sc[pl.ds(row, SUBLANES), half:2 * half]
            return ar * xr - ai * xi + sr, ar * xi + ai * xr + si

        zero = jnp.zeros((SUBLANES, half), F32)
        er, ei = lax.fori_loop(0, ngrp, step, (zero, zero))

        a2r = a2r_ref[0]
        a2i = a2i_ref[0]
        sub = lax.broadcasted_iota(jnp.int32, (SUBLANES, half), 0)
        zr = jnp.zeros((1, half), F32)
        zi = jnp.zeros((1, half), F32)
        zr_all = zero
        zi_all = zero
        for jj in range(1, SUBLANES):
            nzr = a2r * zr - a2i * zi + er[jj - 1:jj]
            nzi = a2r * zi + a2i * zr + ei[jj - 1:jj]
            zr, zi = nzr, nzi
            zr_all = jnp.where(sub == jj, zr, zr_all)
            zi_all = jnp.where(sub == jj, zi, zi_all)

        def fix(r, carry):
            row = pl.multiple_of(r * SUBLANES, SUBLANES)
            pr = pr_ref[0, pl.ds(r, 1), :]
            pi = pi_ref[0, pl.ds(r, 1), :]
            x_sc[pl.ds(row, SUBLANES), 0:half] = (
                x_sc[pl.ds(row, SUBLANES), 0:half] + pr * zr_all - pi * zi_all)
            x_sc[pl.ds(row, SUBLANES), half:2 * half] = (
                x_sc[pl.ds(row, SUBLANES), half:2 * half] + pr * zi_all + pi * zr_all)
            return carry

        lax.fori_loop(0, ngrp, fix, 0)

    base = base_ref[0].astype(BF16)
    mask1 = same_group((LANES, 2 * LANES), lg_h, LANES - 1, lg_h)
    for s in range(T):
        m1_sc[s * LANES:(s + 1) * LANES, :] = (_dot(base, rs_ref[s]) * mask1).astype(BF16)
    mask2 = same_group((half, 2 * LANES), lg_p, LANES - 1, lg_h)
    m2_sc[0:half, :] = (_dot(w2r_ref[0].astype(BF16), rs_ref[0]) * mask2).astype(BF16)
    m2_sc[half:2 * half, :] = (-_dot(w2i_ref[0].astype(BF16), rs_ref[0]) * mask2).astype(BF16)

    y = _dot(ucat_sc[...], m1_sc[...]) + _dot(x_sc[...].astype(BF16), m2_sc[...])
    o_ref[0] = y[:, 0:LANES].astype(BF16)
    o_ref[1] = y[:, LANES:2 * LANES].astype(BF16)


def _ssm(u_s, base, w2_re, w2_im, v3_re, v3_im, rs, r3, a_re, a_im, p_re, p_im, a2_re, a2_im):
    T, nrow, width = u_s.shape
    nb = width // LANES
    half = r3.shape[1]
    nstate = 2 * half
    ngrp = nrow // SUBLANES
    th = rs.shape[1]
    P = r3.shape[0]
    kern = functools.partial(_ssm_kernel, nrow=nrow)
    return pl.pallas_call(
        kern,
        grid=(nb, T // 2),
        in_specs=[pl.BlockSpec((T, nrow, LANES), lambda b, t: (0, 0, b)),
                  pl.BlockSpec((1, LANES, th), lambda b, t: (b, 0, 0)),
                  pl.BlockSpec((1, half, th), lambda b, t: (b, 0, 0)),
                  pl.BlockSpec((1, half, th), lambda b, t: (b, 0, 0)),
                  pl.BlockSpec((1, T * LANES, P), lambda b, t: (b, 0, 0)),
                  pl.BlockSpec((1, T * LANES, P), lambda b, t: (b, 0, 0)),
                  pl.BlockSpec((T, th, 2 * LANES), lambda b, t: (0, 0, t)),
                  pl.BlockSpec((P, half), lambda b, t: (0, 0)),
                  pl.BlockSpec((1, 1, half), lambda b, t: (b, 0, 0)),
                  pl.BlockSpec((1, 1, half), lambda b, t: (b, 0, 0)),
                  pl.BlockSpec((1, ngrp, half), lambda b, t: (b, 0, 0)),
                  pl.BlockSpec((1, ngrp, half), lambda b, t: (b, 0, 0)),
                  pl.BlockSpec((1, 1, half), lambda b, t: (b, 0, 0)),
                  pl.BlockSpec((1, 1, half), lambda b, t: (b, 0, 0))],
        out_specs=pl.BlockSpec((2, nrow, LANES), lambda b, t: (t, 0, b)),
        out_shape=jax.ShapeDtypeStruct((T, nrow, width), BF16),
        scratch_shapes=[pltpu.VMEM((nrow, T * LANES), BF16),
                        pltpu.VMEM((nrow, nstate), F32),
                        pltpu.VMEM((nrow, nstate), F32),
                        pltpu.VMEM((T * LANES, 2 * LANES), BF16),
                        pltpu.VMEM((nstate, 2 * LANES), BF16),
                        pltpu.VMEM((T * LANES, nstate), BF16)],
        compiler_params=_cparams(2, 48),
        name="ssm",
    )(u_s, base, w2_re, w2_im, v3_re, v3_im, rs, r3, a_re, a_im, p_re, p_im, a2_re, a2_im)


def _ssm_tables(lam_re, lam_im, log_dt, b_re, b_im, c_re, c_im, d_skip, ngrp):
    G, P = lam_re.shape
    H = SSM_GROUP
    T = SSM_CHUNK
    gb = SSM_BLOCK_GROUPS
    nb = G // gb
    dt = jnp.exp(log_dt)[:, None]
    mag = jnp.exp(lam_re * dt)
    lb_re = mag * jnp.cos(lam_im * dt)
    lb_im = mag * jnp.sin(lam_im * dt)
    den = lam_re * lam_re + lam_im * lam_im
    nr = lb_re - 1.0
    q_re = (nr * lam_re + lb_im * lam_im) / den
    q_im = (lb_im * lam_re - nr * lam_im) / den
    bb_re = q_re[..., None] * b_re - q_im[..., None] * b_im
    bb_im = q_re[..., None] * b_im + q_im[..., None] * b_re

    def powers(n):
        nf = n.astype(F32)[..., None, None]
        m = jnp.exp(nf * (lam_re * dt))
        return m * jnp.cos(nf * (lam_im * dt)), m * jnp.sin(nf * (lam_im * dt))

    pw_re, pw_im = powers(jnp.arange(T + 1))
    aT_re, aT_im = pw_re[T], pw_im[T]
    pT_re, pT_im = powers(T * jnp.arange(ngrp))
    a2_re, a2_im = powers(jnp.asarray(T * ngrp))

    bbt_re = jnp.transpose(bb_re, (0, 2, 1))
    bbt_im = jnp.transpose(bb_im, (0, 2, 1))
    taps = _taps(c_re, c_im,
                 jnp.transpose(pw_re[:T], (1, 0, 2)), jnp.transpose(pw_im[:T], (1, 0, 2)),
                 bbt_re, bbt_im)
    jj = jnp.arange(H)
    taps = taps.at[:, jj, jj].add(d_skip)
    base = taps.reshape(nb, gb * H, T * H)

    er = pw_re[:T][::-1][:, :, None, :]
    ei = pw_im[:T][::-1][:, :, None, :]
    v_re = er * bbt_re[None] - ei * bbt_im[None]
    v_im = er * bbt_im[None] + ei * bbt_re[None]

    def rows3(v):
        v = v.reshape(T, nb, gb * H, P)
        return jnp.transpose(v, (1, 0, 2, 3)).reshape(nb, T * LANES, P)

    fr = jnp.transpose(pw_re[1:T + 1], (1, 2, 0))[..., None]
    fi = jnp.transpose(pw_im[1:T + 1], (1, 2, 0))[..., None]
    ct_re = jnp.transpose(c_re, (0, 2, 1))[:, :, None, :]
    ct_im = jnp.transpose(c_im, (0, 2, 1))[:, :, None, :]
    w2_re = (ct_re * fr - ct_im * fi).reshape(nb, gb * P, T * H)
    w2_im = (ct_re * fi + ct_im * fr).reshape(nb, gb * P, T * H)

    rr = jnp.arange(T * H)
    cc = jnp.arange(T * LANES)
    same_i = (rr % H)[None, :, None] == (cc % H)[None, None, :]
    shifted = ((rr // H)[None, :, None] + jnp.arange(T)[:, None, None]
               == (cc // LANES)[None, None, :])
    rs = (same_i & shifted).astype(BF16)
    r3 = (jnp.arange(P)[:, None] == (jnp.arange(gb * P) % P)[None, :]).astype(BF16)

    def lanes(a):
        lead = a.shape[:-2]
        a = a.reshape(lead + (nb, gb * P))
        return jnp.moveaxis(a, -2, 0)

    a_re = lanes(aT_re)[:, None, :]
    a_im = lanes(aT_im)[:, None, :]
    p_re = lanes(pT_re)
    p_im = lanes(pT_im)
    a2_re = lanes(a2_re)[:, None, :]
    a2_im = lanes(a2_im)[:, None, :]
    return (base, w2_re, w2_im, rows3(v_re), rows3(v_im), rs, r3,
            a_re, a_im, p_re, p_im, a2_re, a2_im)


def _mixout_kernel(*refs, glu):
    if glu:
        (a_ref, y_ref, x_ref, wglu_ref, bglu_ref, wo_ref, g_ref, lng_ref, lnb_ref,
         o_ref) = refs
        y = y_ref[...].astype(F32)
        s = y * (0.5 * (1.0 + jnp.tanh(math.sqrt(2.0 / math.pi)
                                       * (y + 0.044715 * (y * y * y)))))
        z = _dot(s.astype(BF16), wglu_ref[...]) + bglu_ref[...]
        s2 = (s * _sigmoid(z)).astype(BF16)
        wa = a_ref.shape[1]
        mix = _dot(a_ref[...], wo_ref[0:wa, :]) + _dot(s2, wo_ref[wa:, :])
    else:
        a_ref, x_ref, wo_ref, g_ref, lng_ref, lnb_ref, o_ref = refs
        mix = _dot(a_ref[...], wo_ref[...])
    r = DEEPNORM_ALPHA * x_ref[...] + g_ref[...] * mix
    o_ref[...] = _layer_norm(r, lng_ref[...], lnb_ref[...])


def _mixout(a, y, x, wglu, bglu, wo, g, lng, lnb):
    L, d = x.shape
    tm = 512
    glu = y is not None
    row = lambda i: (i, 0)
    fix = lambda i: (0, 0)
    vec = pl.BlockSpec((1, d), fix)
    if glu:
        ws = y.shape[1]
        args = (a, y, x, wglu, bglu, wo, g, lng, lnb)
        in_specs = [pl.BlockSpec((tm, a.shape[1]), row), pl.BlockSpec((tm, ws), row),
                    pl.BlockSpec((tm, d), row), pl.BlockSpec((ws, ws), fix),
                    pl.BlockSpec((1, ws), fix), pl.BlockSpec(wo.shape, fix), vec, vec, vec]
    else:
        args = (a, x, wo, g, lng, lnb)
        in_specs = [pl.BlockSpec((tm, a.shape[1]), row), pl.BlockSpec((tm, d), row),
                    pl.BlockSpec(wo.shape, fix), vec, vec, vec]
    return pl.pallas_call(
        functools.partial(_mixout_kernel, glu=glu),
        grid=(L // tm,),
        in_specs=in_specs,
        out_specs=pl.BlockSpec((tm, d), row),
        out_shape=jax.ShapeDtypeStruct((L, d), F32),
        compiler_params=_cparams(1, 48),
        name="mixout_glu" if glu else "mixout",
    )(*args)


def _ffn_kernel(x_ref, sc_ref, sh_ref, wg_ref, wu_ref, wd_ref, g_ref, lng_ref, lnb_ref,
                o_ref, h_sc):
    j = pl.program_id(1)

    @pl.when(j == 0)
    def _():
        h_sc[...] = (x_ref[...] * (1.0 + sc_ref[...]) + sh_ref[...]).astype(BF16)
        o_ref[...] = jnp.zeros_like(o_ref)

    h = h_sc[...]
    gt = _dot(h, wg_ref[0])
    ut = _dot(h, wu_ref[0])
    act = (gt * _sigmoid(gt) * ut).astype(BF16)
    o_ref[...] += _dot(act, wd_ref[0])

    @pl.when(j == pl.num_programs(1) - 1)
    def _():
        r = DEEPNORM_ALPHA * x_ref[...] + g_ref[...] * o_ref[...]
        o_ref[...] = _layer_norm(r, lng_ref[...], lnb_ref[...])


def _ffn(x, sc, sh, wg, wu, wd, layer, g, lng, lnb):
    L, d = x.shape
    dff = wg.shape[2]
    tm, tf = 1024, 256
    row = lambda i, j: (i, 0)
    vec = pl.BlockSpec((1, d), lambda i, j: (0, 0))
    return pl.pallas_call(
        _ffn_kernel,
        grid=(L // tm, dff // tf),
        in_specs=[pl.BlockSpec((tm, d), row), vec, vec,
                  pl.BlockSpec((1, d, tf), lambda i, j: (layer, 0, j)),
                  pl.BlockSpec((1, d, tf), lambda i, j: (layer, 0, j)),
                  pl.BlockSpec((1, tf, d), lambda i, j: (layer, j, 0)),
                  vec, vec, vec],
        out_specs=pl.BlockSpec((tm, d), row),
        out_shape=jax.ShapeDtypeStruct((L, d), F32),
        scratch_shapes=[pltpu.VMEM((tm, d), BF16)],
        compiler_params=_cparams(2, 58),
        name="ffn",
    )(x, sc, sh, wg, wu, wd, g, lng, lnb)


def _inproj1_kernel(x_ref, sc_ref, sh_ref, posc_ref, posr_ref, freq_ref, slo_ref, shi_ref,
                    fcol_ref, wqt_ref, wk_ref, wvt_ref, qt_ref, k_ref, vt_ref,
                    *, slab, scale):
    half = ROT_DIM // 2
    dh = SWA_HEAD_DIM
    hb = (x_ref[...] * (1.0 + sc_ref[...]) + sh_ref[...]).astype(BF16)

    ang = fcol_ref[...] * posr_ref[...]
    cs = jnp.cos(ang)
    sn = jnp.sin(ang)
    for c in range(qt_ref.shape[0] // slab):
        r = _dot_nt(wqt_ref[c * slab:(c + 1) * slab, :], hb)
        for h in range(slab // dh):
            b = h * dh
            x1 = r[b:b + half]
            x2 = r[b + half:b + ROT_DIM]
            rot = jnp.concatenate([x1 * cs - x2 * sn, x2 * cs + x1 * sn], axis=0)
            qt_ref[c * slab + b:c * slab + b + ROT_DIM, :] = (rot * scale).astype(BF16)
            qt_ref[c * slab + b + ROT_DIM:c * slab + b + dh, :] = (
                r[b + ROT_DIM:b + dh] * scale).astype(BF16)

    kk = _dot(hb, wk_ref[...])
    ang = posc_ref[...] * freq_ref[...]
    cs = jnp.cos(ang)
    sn = jnp.sin(ang)
    sin_lo = sn * slo_ref[...]
    sin_hi = sn * shi_ref[...]
    for c in range(kk.shape[1] // LANES):
        xc = kk[:, c * LANES:(c + 1) * LANES]
        k_ref[:, c * LANES:(c + 1) * LANES] = (
            xc * cs + pltpu.roll(xc, LANES - half, 1) * sin_lo
            + pltpu.roll(xc, half, 1) * sin_hi).astype(BF16)
    vt_ref[...] = _dot_nt(wvt_ref[...], hb).astype(BF16)


def _inproj1(x, sc, sh, pos, w_qt, w_k, w_vt):
    L, d = x.shape
    n_q = w_qt.shape[0]
    n_k = w_k.shape[1]
    tm = 512
    half = ROT_DIM // 2
    inv_freq = jnp.power(jnp.float32(ROPE_THETA), -jnp.arange(half, dtype=F32) * (2.0 / ROT_DIM))
    dlane = jnp.arange(LANES) % SWA_HEAD_DIM
    freq = jnp.where(dlane < ROT_DIM, inv_freq[dlane % half], 0.0).reshape(1, LANES)
    slo = jnp.where(dlane < half, -1.0, 0.0).astype(F32).reshape(1, LANES)
    shi = jnp.where((dlane >= half) & (dlane < ROT_DIM), 1.0, 0.0).astype(F32).reshape(1, LANES)
    posf = pos.astype(F32)
    kern = functools.partial(_inproj1_kernel, slab=512, scale=LOG2E / math.sqrt(SWA_HEAD_DIM))

    def const(shape):
        return pl.BlockSpec(shape, lambda i: (0, 0), pipeline_mode=pl.Buffered(1))

    return pl.pallas_call(
        kern,
        grid=(L // tm,),
        in_specs=[pl.BlockSpec((tm, d), lambda i: (i, 0)),
                  const((1, d)), const((1, d)),
                  pl.BlockSpec((tm, 1), lambda i: (i, 0)),
                  pl.BlockSpec((1, tm), lambda i: (0, i)),
                  const((1, LANES)), const((1, LANES)), const((1, LANES)), const((half, 1)),
                  const((n_q, d)), const((d, n_k)), const((n_k, d))],
        out_specs=[pl.BlockSpec((n_q, tm), lambda i: (0, i)),
                   pl.BlockSpec((tm, n_k), lambda i: (i, 0)),
                   pl.BlockSpec((n_k, tm), lambda i: (0, i))],
        out_shape=[jax.ShapeDtypeStruct((n_q, L), BF16),
                   jax.ShapeDtypeStruct((L, n_k), BF16),
                   jax.ShapeDtypeStruct((n_k, L), BF16)],
        compiler_params=_cparams(1, 40),
        name="inproj1",
    )(x, sc, sh, posf.reshape(L, 1), posf.reshape(1, L), freq, slo, shi,
      inv_freq.reshape(half, 1), w_qt, w_k, w_vt)


SWA_BLOCKS_PER_STEP = 4


def _swa_kernel(sink_ref, qt_ref, kc_ref, kp_ref, vtc_ref, vtp_ref, o_ref, *, n_kv):
    n = pl.program_id(0)
    W = WINDOW
    dh = SWA_HEAD_DIM
    G = SWA_Q_PER_KV
    kj = lax.broadcasted_iota(jnp.int32, (2 * W, W), 0)
    qi = lax.broadcasted_iota(jnp.int32, (2 * W, W), 1)
    for qb in range(SWA_BLOCKS_PER_STEP):
        cur = slice(qb * W, (qb + 1) * W)
        prev = slice((qb - 1) * W, qb * W)
        lo_key = jnp.where(n > 0, qi, W - 1) if qb == 0 else qi
        band = jnp.logical_and(kj > lo_key, kj <= qi + W)
        mask1 = jnp.where(band, 0.0, NEG_BIG)
        mask = jnp.concatenate([mask1] * G, axis=1)
        for kvh in range(n_kv):
            ksl = slice(kvh * dh, (kvh + 1) * dh)
            k_prev = kp_ref[:, ksl] if qb == 0 else kc_ref[prev, ksl]
            vt_prev = vtp_ref[ksl, :] if qb == 0 else vtc_ref[ksl, prev]
            kk = jnp.concatenate([k_prev, kc_ref[cur, ksl]], axis=0)
            vv = jnp.concatenate([vt_prev, vtc_ref[ksl, cur]], axis=1)
            q = jnp.concatenate([qt_ref[(kvh * G + g) * dh:(kvh * G + g + 1) * dh, cur]
                                 for g in range(G)], axis=1)
            s = _dot(kk, q) + mask
            sink = jnp.concatenate([jnp.full((1, W), sink_ref[kvh * G + g] * LOG2E, F32)
                                    for g in range(G)], axis=1)
            m = jnp.maximum(jnp.max(s, axis=0, keepdims=True), sink)
            p = jnp.exp2(s - m)
            den = jnp.sum(p, axis=0, keepdims=True) + jnp.exp2(sink - m)
            o = _dot(vv, p.astype(BF16)) / den
            for gp in range(G // 2):
                two = jnp.concatenate([o[:, (2 * gp) * W:(2 * gp + 1) * W],
                                       o[:, (2 * gp + 1) * W:(2 * gp + 2) * W]], axis=0)
                lo = (kvh * G + 2 * gp) * dh
                o_ref[cur, lo:lo + 2 * dh] = two.T.astype(BF16)


def _swa(qt, k, vt, sinks):
    n_q, L = qt.shape
    kw = k.shape[1]
    n_kv = kw // SWA_HEAD_DIM
    W = WINDOW
    nb = SWA_BLOCKS_PER_STEP
    S = nb * W
    prev_blk = lambda n: jnp.maximum(nb * n - 1, 0)
    grid_spec = pltpu.PrefetchScalarGridSpec(
        num_scalar_prefetch=1,
        grid=(L // S,),
        in_specs=[pl.BlockSpec((n_q, S), lambda n, s: (0, n)),
                  pl.BlockSpec((S, kw), lambda n, s: (n, 0)),
                  pl.BlockSpec((W, kw), lambda n, s: (prev_blk(n), 0)),
                  pl.BlockSpec((kw, S), lambda n, s: (0, n)),
                  pl.BlockSpec((kw, W), lambda n, s: (0, prev_blk(n)))],
        out_specs=pl.BlockSpec((S, n_q), lambda n, s: (n, 0)),
    )
    return pl.pallas_call(
        functools.partial(_swa_kernel, n_kv=n_kv),
        grid_spec=grid_spec,
        out_shape=jax.ShapeDtypeStruct((L, n_q), BF16),
        compiler_params=_cparams(1, 32),
        name="swa",
    )(sinks, qt, k, k, vt, vt)


def kernel(x, c, positions, w_in_ab, b_forget, ssm_lambda_re, ssm_lambda_im, ssm_log_dt, ssm_b_re, ssm_b_im, ssm_c_re, ssm_c_im, ssm_d, w_glu, b_glu, w_out_ab, w_in_c, attn_sinks, w_out_c, w_ada, b_ada, ln_mix_g, ln_mix_b, ln_ffn_g, ln_ffn_b, w_ffn_gate, w_ffn_up, w_ffn_down):
    B, L, D = x.shape
    assert B == 1
    x2 = x.reshape(L, D)
    mod = _ada(c, w_ada, b_ada)

    def mods(layer):
        return [mod[layer, :, k * D:(k + 1) * D] for k in range(6)]

    def vec(a):
        return a.reshape(1, -1)

    sh1, sc1, g1, sh2, sc2, g2 = mods(0)
    W = FOX_WIDTH
    w_in = w_in_ab[0]
    w_qku = jnp.concatenate([w_in[:, :2 * W], w_in[:, 3 * W + FOX_HEADS:]], 1).astype(BF16)
    w_vt = w_in[:, 2 * W:3 * W].T.astype(BF16)
    w_f = jnp.pad(w_in[:, 3 * W:3 * W + FOX_HEADS], ((0, 0), (0, LANES - FOX_HEADS))).astype(BF16)
    b_f = jnp.pad(b_forget[0], (0, LANES - FOX_HEADS)).reshape(1, LANES)
    tq = 512
    qa, ka, u, vt = _inproj0(x2, sc1, sh1, w_qku, w_vt, w_f, b_f, tq)
    a_out = _fox(qa, ka, vt, tq)

    T = SSM_CHUNK
    nrow = L // T
    nsub = nrow // SUBLANES
    width = u.shape[1]
    u_s = u.reshape(SUBLANES, nsub, T, width).transpose(2, 1, 0, 3).reshape(T, nrow, width)
    tables = _ssm_tables(ssm_lambda_re[0], ssm_lambda_im[0], ssm_log_dt[0], ssm_b_re[0],
                         ssm_b_im[0], ssm_c_re[0], ssm_c_im[0], ssm_d[0], nsub)
    y_s = _ssm(u_s, *tables)
    y = y_s.reshape(T, nsub, SUBLANES, width).transpose(2, 1, 0, 3).reshape(L, width)

    x2 = _mixout(a_out, y, x2, w_glu[0].astype(BF16), vec(b_glu[0]), w_out_ab[0].astype(BF16),
                 g1, vec(ln_mix_g[0]), vec(ln_mix_b[0]))
    wg, wu, wd = (w.astype(BF16) for w in (w_ffn_gate, w_ffn_up, w_ffn_down))
    x2 = _ffn(x2, sc2, sh2, wg, wu, wd, 0, g2, vec(ln_ffn_g[0]), vec(ln_ffn_b[0]))

    sh1, sc1, g1, sh2, sc2, g2 = mods(1)
    n_q = w_out_c.shape[1]
    n_k = (w_in_c.shape[2] - n_q) // 2
    w_c = w_in_c[0]
    a1 = _swa(*_inproj1(x2, sc1, sh1, positions, w_c[:, :n_q].T.astype(BF16),
                        w_c[:, n_q:n_q + n_k].astype(BF16), w_c[:, n_q + n_k:].T.astype(BF16)),
              attn_sinks[0])
    x2 = _mixout(a1, None, x2, None, None, w_out_c[0].astype(BF16), g1,
                 vec(ln_mix_g[1]), vec(ln_mix_b[1]))
    x2 = _ffn(x2, sc2, sh2, wg, wu, wd, 1, g2, vec(ln_ffn_g[1]), vec(ln_ffn_b[1]))
    return x2.reshape(B, L, D)
```

```python
import functools
import math

import jax
import jax.numpy as jnp
from jax import lax
from jax.experimental import pallas as pl
from jax.experimental.pallas import tpu as pltpu

F32 = jnp.float32
BF16 = jnp.bfloat16

FOX_HEADS = 8
FOX_HEAD_DIM = 128
FOX_WIDTH = FOX_HEADS * FOX_HEAD_DIM
SSM_GROUP = 16
SSM_STATE = 64
SWA_HEAD_DIM = 64
SWA_Q_PER_KV = 8
WINDOW = 128
ROT_DIM = SWA_HEAD_DIM // 4
ROPE_THETA = 500000.0
DEPTH = 2
DEEPNORM_ALPHA = (2 * DEPTH) ** 0.25
LN_EPS = 1e-5

LANES = 128
SUBLANES = 8
NEG_BIG = -1e30

SSM_CHUNK = 16
SSM_BLOCK_GROUPS = LANES // SSM_GROUP


def _cparams(n_axes, vmem_mb):
    return pltpu.CompilerParams(
        dimension_semantics=("arbitrary",) * n_axes,
        vmem_limit_bytes=vmem_mb * 1024 * 1024)


def _sigmoid(x):
    return 1.0 / (1.0 + jnp.exp(-x))


def _layer_norm(r, g, b):
    mu = jnp.mean(r, axis=-1, keepdims=True)
    d = r - mu
    var = jnp.mean(d * d, axis=-1, keepdims=True)
    return d * lax.rsqrt(var + LN_EPS) * g + b


def _dot(a, b):
    return jnp.dot(a, b, preferred_element_type=F32)


def _dot_nt(a, b):
    return lax.dot_general(a, b, (((1,), (1,)), ((), ())), preferred_element_type=F32)


def _ada_kernel(c_ref, w_ref, b_ref, o_ref):
    d, tn = w_ref.shape[1], w_ref.shape[2]
    c = c_ref[...]
    sc = c * _sigmoid(c)
    sc = jnp.concatenate([sc] * (tn // LANES), axis=1)
    part = jnp.sum((w_ref[0] * sc).reshape(d // SUBLANES, SUBLANES, tn), axis=0)
    row = jnp.sum(part, axis=0, keepdims=True) + b_ref[0]
    o_ref[0] = jnp.broadcast_to(row, (SUBLANES, tn))


def _ada(c, w_ada, b_ada):
    depth, d, n = w_ada.shape
    tn = 1536
    c_lanes = jnp.broadcast_to(c.reshape(d, 1), (d, LANES))
    out = pl.pallas_call(
        _ada_kernel,
        grid=(depth, n // tn),
        in_specs=[pl.BlockSpec((d, LANES), lambda l, j: (0, 0)),
                  pl.BlockSpec((1, d, tn), lambda l, j: (l, 0, j)),
                  pl.BlockSpec((1, 1, tn), lambda l, j: (l, 0, j))],
        out_specs=pl.BlockSpec((1, SUBLANES, tn), lambda l, j: (l, 0, j)),
        out_shape=jax.ShapeDtypeStruct((depth, SUBLANES, n), F32),
        compiler_params=_cparams(2, 40),
        name="ada",
    )(c_lanes, w_ada, b_ada.reshape(depth, 1, n))
    return out[:, 0:1, :]


N_SPLIT = 3
ONES_ROWS = 16
LOG2E = 1.0 / math.log(2.0)


def _inproj0_kernel(x_ref, sc_ref, sh_ref, w_ref, wvt_ref, wf_ref, bf_ref, pq_ref, pk_ref,
                    cq_ref, ck_ref, qa_ref, ka_ref, u_ref, vt_ref, carry_sc,
                    *, tm, scale):
    i = pl.program_id(0)
    hd = FOX_HEAD_DIM
    W = FOX_WIDTH

    def put_heads(dst_ref, val, off):
        for h in range(FOX_HEADS):
            dst_ref[:, 2 * h * hd + off:2 * h * hd + off + hd] = (
                val[:, h * hd:(h + 1) * hd].astype(BF16))

    hb = (x_ref[...] * (1.0 + sc_ref[...]) + sh_ref[...]).astype(BF16)
    z = _dot(hb, wf_ref[...]) + bf_ref[...]
    lf = jnp.minimum(z, 0.0) - jnp.log1p(jnp.exp(-jnp.abs(z)))

    def split(v):
        p1 = v.astype(BF16)
        r1 = v - p1.astype(F32)
        p2 = r1.astype(BF16)
        return p1, p2, (r1 - p2.astype(F32)).astype(BF16)

    row = lax.broadcasted_iota(jnp.int32, (tm, tm), 0)
    col = lax.broadcasted_iota(jnp.int32, (tm, tm), 1)
    tri = (col <= row).astype(BF16)
    cum = sum(_dot(tri, p) for p in split(lf))

    @pl.when(i == 0)
    def _():
        carry_sc[...] = jnp.zeros_like(carry_sc)

    f = cum + carry_sc[...]
    carry_sc[...] = f[tm - 1:tm, :]
    fp = jnp.concatenate(split(f * LOG2E), axis=1)
    put_heads(qa_ref, _dot(fp, pq_ref[...]) + cq_ref[...], hd)
    put_heads(ka_ref, _dot(fp, pk_ref[...]) + ck_ref[...], hd)
    put_heads(qa_ref, _dot(hb, w_ref[:, 0:W]) * scale, 0)
    put_heads(ka_ref, _dot(hb, w_ref[:, W:2 * W]), 0)
    u_ref[...] = _dot(hb, w_ref[:, 2 * W:3 * W]).astype(BF16)
    vt = _dot_nt(wvt_ref[...], hb)
    vt_ref[:, :, 0:hd, :] = vt.astype(BF16).reshape(FOX_HEADS, 1, hd, tm)
    vt_ref[:, :, hd:, :] = jnp.ones((FOX_HEADS, 1, ONES_ROWS, tm), BF16)


def _inproj0(x, sc, sh, w_qku, w_vt, wf, bf, tm):
    L, d = x.shape
    W = FOX_WIDTH
    hd = FOX_HEAD_DIM
    piece = jnp.arange(N_SPLIT)
    head = jnp.arange(FOX_HEADS)
    rows = (piece[:, None] * LANES + head[None, :]).reshape(-1)
    cols_k = (head[None, :] * hd + piece[:, None]).reshape(-1)
    cols_q = cols_k + N_SPLIT
    pq = jnp.zeros((N_SPLIT * LANES, W), F32).at[rows, cols_q].set(1.0).astype(BF16)
    pk = jnp.zeros((N_SPLIT * LANES, W), F32).at[rows, cols_k].set(-1.0).astype(BF16)
    lane = jnp.arange(W) % hd
    cq = (lane < N_SPLIT).astype(F32).reshape(1, W)
    ck = ((lane >= N_SPLIT) & (lane < 2 * N_SPLIT)).astype(F32).reshape(1, W)
    kern = functools.partial(_inproj0_kernel, tm=tm, scale=LOG2E / math.sqrt(hd))
    hv = hd + ONES_ROWS
    row = lambda i: (i, 0)

    def const(shape):
        return pl.BlockSpec(shape, lambda i: (0, 0), pipeline_mode=pl.Buffered(1))

    return pl.pallas_call(
        kern,
        grid=(L // tm,),
        in_specs=[pl.BlockSpec((tm, d), row),
                  const((1, d)), const((1, d)),
                  const((d, 3 * W)), const((W, d)),
                  const((d, LANES)), const((1, LANES)),
                  const((N_SPLIT * LANES, W)), const((N_SPLIT * LANES, W)),
                  const((1, W)), const((1, W))],
        out_specs=[pl.BlockSpec((tm, 2 * W), row),
                   pl.BlockSpec((tm, 2 * W), row),
                   pl.BlockSpec((tm, W), row),
                   pl.BlockSpec((FOX_HEADS, 1, hv, tm), lambda i: (0, i, 0, 0))],
        out_shape=[jax.ShapeDtypeStruct((L, 2 * W), BF16),
                   jax.ShapeDtypeStruct((L, 2 * W), BF16),
                   jax.ShapeDtypeStruct((L, W), BF16),
                   jax.ShapeDtypeStruct((FOX_HEADS, L // tm, hv, tm), BF16)],
        scratch_shapes=[pltpu.VMEM((1, LANES), F32)],
        compiler_params=_cparams(1, 48),
        name="inproj0",
    )(x, sc, sh, w_qku, w_vt, wf, bf, pq, pk, cq, ck)


M_INIT = -1e29


def _fox_kernel(qa_ref, ka_ref, vt_ref, o_ref, s0, s1, p0, p1, al0, al1, mb0, mb1, m_sc,
                acc_sc, mask_sc, *, tq, tk):
    first = jnp.logical_and(pl.program_id(0) == 0, pl.program_id(1) == 0)
    i = pl.program_id(1)
    last = 2 * i + 1

    @pl.when(first)
    def _():
        key = lax.broadcasted_iota(jnp.int32, (tk, tq), 0)
        qry = lax.broadcasted_iota(jnp.int32, (tk, tq), 1)
        mask_sc[0] = jnp.zeros((tk, tq), F32)
        mask_sc[1] = jnp.where(key <= qry, 0.0, NEG_BIG)
        mask_sc[2] = jnp.where(key + tk <= qry, 0.0, NEG_BIG)

    q = qa_ref[...]
    hd = o_ref.shape[1]
    s1[...] = jnp.full_like(s1, NEG_BIG)
    mb1[...] = jnp.full_like(mb1, NEG_BIG)
    p0[...] = jnp.zeros_like(p0)
    al0[...] = jnp.ones_like(al0)
    m_sc[...] = jnp.full_like(m_sc, M_INIT)
    acc_sc[...] = jnp.zeros_like(acc_sc)

    halves = [slice(c * (tq // 4), (c + 1) * (tq // 4)) for c in range(4)]

    def scores(t, s_ref, mb_ref, masked):
        start = pl.multiple_of(t * tk, tk)
        kb = ka_ref[pl.ds(start, tk), :]
        sel = jnp.clip(t - last + 2, 0, 2)
        for qs in halves:
            s = _dot_nt(kb, q[qs])
            if masked:
                s = s + mask_sc[sel, :, qs]
            s_ref[:, qs] = s
            mb_ref[:, qs] = jnp.max(s, axis=0, keepdims=True)

    def softmax(s_ref, mb_ref, p_ref, al_ref):
        m_prev = m_sc[...]
        m_new = jnp.maximum(m_prev, mb_ref[...])
        m_sc[...] = m_new
        al_ref[...] = jnp.exp2(m_prev - m_new)
        for qs in halves:
            p_ref[:, qs] = jnp.exp2(s_ref[:, qs] - m_new[:, qs]).astype(BF16)

    def values(t, p_ref, al_ref):
        vb = vt_ref[0, jnp.maximum(t, 0)]
        for qs in halves:
            acc_sc[:, qs] = al_ref[:, qs] * acc_sc[:, qs] + _dot(vb, p_ref[:, qs])

    def pair(u, carry, masked):
        t = 2 * u
        scores(t, s0, mb0, masked)
        softmax(s1, mb1, p1, al1)
        values(t - 2, p0, al0)
        scores(t + 1, s1, mb1, masked)
        softmax(s0, mb0, p0, al0)
        values(t - 1, p1, al1)
        return carry

    def quad(v, carry):
        pair(2 * v, carry, False)
        return pair(2 * v + 1, carry, False)

    n_quad = i // 2
    lax.fori_loop(0, n_quad, quad, 0)
    lax.fori_loop(2 * n_quad, i + 1, functools.partial(pair, masked=True), 0)
    softmax(s1, mb1, p1, al1)
    values(last - 1, p0, al0)
    values(last, p1, al1)

    acc = acc_sc[...]
    o_ref[...] = (acc[0:hd] / acc[hd:hd + 1]).T.astype(BF16)


def _fox(qa, ka, vt, tk):
    L = qa.shape[0]
    hd = FOX_HEAD_DIM
    hv = hd + ONES_ROWS
    tq = 2 * tk
    assert vt.shape == (FOX_HEADS, L // tk, hv, tk)
    row = pltpu.VMEM((1, tq), F32)
    return pl.pallas_call(
        functools.partial(_fox_kernel, tq=tq, tk=tk),
        grid=(FOX_HEADS, L // tq),
        in_specs=[pl.BlockSpec((tq, 2 * hd), lambda h, i: (i, h)),
                  pl.BlockSpec((L, 2 * hd), lambda h, i: (0, h)),
                  pl.BlockSpec((1, L // tk, hv, tk), lambda h, i: (h, 0, 0, 0))],
        out_specs=pl.BlockSpec((tq, hd), lambda h, i: (i, h)),
        out_shape=jax.ShapeDtypeStruct((L, FOX_WIDTH), BF16),
        scratch_shapes=[pltpu.VMEM((tk, tq), F32), pltpu.VMEM((tk, tq), F32),
                        pltpu.VMEM((tk, tq), BF16), pltpu.VMEM((tk, tq), BF16),
                        row, row, row, row, row,
                        pltpu.VMEM((hv, tq), F32), pltpu.VMEM((3, tk, tq), F32)],
        compiler_params=_cparams(2, 48),
        name="fox",
    )(qa, ka, vt)


def _taps_kernel(cre_ref, cim_ref, pre_ref, pim_ref, bre_ref, bim_ref, o_ref):
    T, H, P = SSM_CHUNK, SSM_GROUP, SSM_STATE
    nt = (((1,), (1,)), ((), ()))
    for g in range(SSM_BLOCK_GROUPS):
        cre = cre_ref[g][None]
        cim = cim_ref[g][None]
        pre = pre_ref[g][:, None, :]
        pim = pim_ref[g][:, None, :]
        wre = (cre * pre - cim * pim).reshape(T * H, P)
        wim = (cre * pim + cim * pre).reshape(T * H, P)
        k = (lax.dot_general(bre_ref[g], wre, nt, precision=lax.Precision.HIGHEST,
                             preferred_element_type=F32)
             - lax.dot_general(bim_ref[g], wim, nt, precision=lax.Precision.HIGHEST,
                               preferred_element_type=F32))
        o_ref[g] = k


def _taps(c_re, c_im, pw_re, pw_im, bbt_re, bbt_im):
    G = c_re.shape[0]
    gb = SSM_BLOCK_GROUPS
    T, H, P = SSM_CHUNK, SSM_GROUP, SSM_STATE
    return pl.pallas_call(
        _taps_kernel,
        grid=(G // gb,),
        in_specs=[pl.BlockSpec((gb, H, P), lambda b: (b, 0, 0)),
                  pl.BlockSpec((gb, H, P), lambda b: (b, 0, 0)),
                  pl.BlockSpec((gb, T, P), lambda b: (b, 0, 0)),
                  pl.BlockSpec((gb, T, P), lambda b: (b, 0, 0)),
                  pl.BlockSpec((gb, H, P), lambda b: (b, 0, 0)),
                  pl.BlockSpec((gb, H, P), lambda b: (b, 0, 0))],
        out_specs=pl.BlockSpec((gb, H, T * H), lambda b: (b, 0, 0)),
        out_shape=jax.ShapeDtypeStruct((G, H, T * H), F32),
        compiler_params=_cparams(1, 32),
        name="taps",
    )(c_re, c_im, pw_re, pw_im, bbt_re, bbt_im)


def _ssm_kernel(u_ref, base_ref, w2r_ref, w2i_ref, v3r_ref, v3i_ref, rs_ref, r3_ref,
                ar_ref, ai_ref, pr_ref, pi_ref, a2r_ref, a2i_ref, o_ref,
                ucat_sc, s_sc, x_sc, m1_sc, m2_sc, m3_sc, *, nrow):
    tp = pl.program_id(1)
    T = SSM_CHUNK
    half = x_sc.shape[1] // 2
    ngrp = nrow // SUBLANES
    lg_h = SSM_GROUP.bit_length() - 1
    lg_p = SSM_STATE.bit_length() - 1

    def same_group(shape, row_shift, col_and, col_shift):
        r = lax.shift_right_logical(lax.broadcasted_iota(jnp.int32, shape, 0), row_shift)
        c = lax.broadcasted_iota(jnp.int32, shape, 1)
        c = lax.shift_right_logical(jnp.bitwise_and(c, col_and), col_shift)
        return (r == c).astype(F32)

    @pl.when(tp == 0)
    def _():
        for s in range(T):
            ucat_sc[:, s * LANES:(s + 1) * LANES] = u_ref[s]
        mask3 = same_group((LANES, half), lg_h, half - 1, lg_p)
        for v_ref, lo in ((v3r_ref, 0), (v3i_ref, half)):
            e = _dot(v_ref[0].astype(BF16), r3_ref[...])
            for s in range(T):
                m3_sc[s * LANES:(s + 1) * LANES, lo:lo + half] = (
                    e[s * LANES:(s + 1) * LANES] * mask3).astype(BF16)
        s_sc[...] = _dot(ucat_sc[...], m3_sc[...])
        ar = ar_ref[0]
        ai = ai_ref[0]

        def step(r, carry):
            xr, xi = carry
            row = pl.multiple_of(r * SUBLANES, SUBLANES)
            x_sc[pl.ds(row, SUBLANES), 0:half] = xr
            x_sc[pl.ds(row, SUBLANES), half:2 * half] = xi
            sr = s_sc[pl.ds(row, SUBLANES), 0:half]
            si = s_sc[pl.ds(row, SUBLANES), half:2 * half]
            return ar * xr - ai * xi + sr, ar * xi + ai * xr + si

        zero = jnp.zeros((SUBLANES, half), F32)
        er, ei = lax.fori_loop(0, ngrp, step, (zero, zero))

        a2r = a2r_ref[0]
        a2i = a2i_ref[0]
        sub = lax.broadcasted_iota(jnp.int32, (SUBLANES, half), 0)
        zr = jnp.zeros((1, half), F32)
        zi = jnp.zeros((1, half), F32)
        zr_all = zero
        zi_all = zero
        for jj in range(1, SUBLANES):
            nzr = a2r * zr - a2i * zi + er[jj - 1:jj]
            nzi = a2r * zi + a2i * zr + ei[jj - 1:jj]
            zr, zi = nzr, nzi
            zr_all = jnp.where(sub == jj, zr, zr_all)
            zi_all = jnp.where(sub == jj, zi, zi_all)

        def fix(r, carry):
            row = pl.multiple_of(r * SUBLANES, SUBLANES)
            pr = pr_ref[0, pl.ds(r, 1), :]
            pi = pi_ref[0, pl.ds(r, 1), :]
            x_sc[pl.ds(row, SUBLANES), 0:half] = (
                x_sc[pl.ds(row, SUBLANES), 0:half] + pr * zr_all - pi * zi_all)
            x_sc[pl.ds(row, SUBLANES), half:2 * half] = (
                x_sc[pl.ds(row, SUBLANES), half:2 * half] + pr * zi_all + pi * zr_all)
            return carry

        lax.fori_loop(0, ngrp, fix, 0)

    base = base_ref[0].astype(BF16)
    mask1 = same_group((LANES, 2 * LANES), lg_h, LANES - 1, lg_h)
    for s in range(T):
        m1_sc[s * LANES:(s + 1) * LANES, :] = (_dot(base, rs_ref[s]) * mask1).astype(BF16)
    mask2 = same_group((half, 2 * LANES), lg_p, LANES - 1, lg_h)
    m2_sc[0:half, :] = (_dot(w2r_ref[0].astype(BF16), rs_ref[0]) * mask2).astype(BF16)
    m2_sc[half:2 * half, :] = (-_dot(w2i_ref[0].astype(BF16), rs_ref[0]) * mask2).astype(BF16)

    y = _dot(ucat_sc[...], m1_sc[...]) + _dot(x_sc[...].astype(BF16), m2_sc[...])
    o_ref[0] = y[:, 0:LANES].astype(BF16)
    o_ref[1] = y[:, LANES:2 * LANES].astype(BF16)


def _ssm(u_s, base, w2_re, w2_im, v3_re, v3_im, rs, r3, a_re, a_im, p_re, p_im, a2_re, a2_im):
    T, nrow, width = u_s.shape
    nb = width // LANES
    half = r3.shape[1]
    nstate = 2 * half
    ngrp = nrow // SUBLANES
    th = rs.shape[1]
    P = r3.shape[0]
    kern = functools.partial(_ssm_kernel, nrow=nrow)
    return pl.pallas_call(
        kern,
        grid=(nb, T // 2),
        in_specs=[pl.BlockSpec((T, nrow, LANES), lambda b, t: (0, 0, b)),
                  pl.BlockSpec((1, LANES, th), lambda b, t: (b, 0, 0)),
                  pl.BlockSpec((1, half, th), lambda b, t: (b, 0, 0)),
                  pl.BlockSpec((1, half, th), lambda b, t: (b, 0, 0)),
                  pl.BlockSpec((1, T * LANES, P), lambda b, t: (b, 0, 0)),
                  pl.BlockSpec((1, T * LANES, P), lambda b, t: (b, 0, 0)),
                  pl.BlockSpec((T, th, 2 * LANES), lambda b, t: (0, 0, t)),
                  pl.BlockSpec((P, half), lambda b, t: (0, 0)),
                  pl.BlockSpec((1, 1, half), lambda b, t: (b, 0, 0)),
                  pl.BlockSpec((1, 1, half), lambda b, t: (b, 0, 0)),
                  pl.BlockSpec((1, ngrp, half), lambda b, t: (b, 0, 0)),
                  pl.BlockSpec((1, ngrp, half), lambda b, t: (b, 0, 0)),
                  pl.BlockSpec((1, 1, half), lambda b, t: (b, 0, 0)),
                  pl.BlockSpec((1, 1, half), lambda b, t: (b, 0, 0))],
        out_specs=pl.BlockSpec((2, nrow, LANES), lambda b, t: (t, 0, b)),
        out_shape=jax.ShapeDtypeStruct((T, nrow, width), BF16),
        scratch_shapes=[pltpu.VMEM((nrow, T * LANES), BF16),
                        pltpu.VMEM((nrow, nstate), F32),
                        pltpu.VMEM((nrow, nstate), F32),
                        pltpu.VMEM((T * LANES, 2 * LANES), BF16),
                        pltpu.VMEM((nstate, 2 * LANES), BF16),
                        pltpu.VMEM((T * LANES, nstate), BF16)],
        compiler_params=_cparams(2, 48),
        name="ssm",
    )(u_s, base, w2_re, w2_im, v3_re, v3_im, rs, r3, a_re, a_im, p_re, p_im, a2_re, a2_im)


def _ssm_tables(lam_re, lam_im, log_dt, b_re, b_im, c_re, c_im, d_skip, ngrp):
    G, P = lam_re.shape
    H = SSM_GROUP
    T = SSM_CHUNK
    gb = SSM_BLOCK_GROUPS
    nb = G // gb
    dt = jnp.exp(log_dt)[:, None]
    mag = jnp.exp(lam_re * dt)
    lb_re = mag * jnp.cos(lam_im * dt)
    lb_im = mag * jnp.sin(lam_im * dt)
    den = lam_re * lam_re + lam_im * lam_im
    nr = lb_re - 1.0
    q_re = (nr * lam_re + lb_im * lam_im) / den
    q_im = (lb_im * lam_re - nr * lam_im) / den
    bb_re = q_re[..., None] * b_re - q_im[..., None] * b_im
    bb_im = q_re[..., None] * b_im + q_im[..., None] * b_re

    def powers(n):
        nf = n.astype(F32)[..., None, None]
        m = jnp.exp(nf * (lam_re * dt))
        return m * jnp.cos(nf * (lam_im * dt)), m * jnp.sin(nf * (lam_im * dt))

    pw_re, pw_im = powers(jnp.arange(T + 1))
    aT_re, aT_im = pw_re[T], pw_im[T]
    pT_re, pT_im = powers(T * jnp.arange(ngrp))
    a2_re, a2_im = powers(jnp.asarray(T * ngrp))

    bbt_re = jnp.transpose(bb_re, (0, 2, 1))
    bbt_im = jnp.transpose(bb_im, (0, 2, 1))
    taps = _taps(c_re, c_im,
                 jnp.transpose(pw_re[:T], (1, 0, 2)), jnp.transpose(pw_im[:T], (1, 0, 2)),
                 bbt_re, bbt_im)
    jj = jnp.arange(H)
    taps = taps.at[:, jj, jj].add(d_skip)
    base = taps.reshape(nb, gb * H, T * H)

    er = pw_re[:T][::-1][:, :, None, :]
    ei = pw_im[:T][::-1][:, :, None, :]
    v_re = er * bbt_re[None] - ei * bbt_im[None]
    v_im = er * bbt_im[None] + ei * bbt_re[None]

    def rows3(v):
        v = v.reshape(T, nb, gb * H, P)
        return jnp.transpose(v, (1, 0, 2, 3)).reshape(nb, T * LANES, P)

    fr = jnp.transpose(pw_re[1:T + 1], (1, 2, 0))[..., None]
    fi = jnp.transpose(pw_im[1:T + 1], (1, 2, 0))[..., None]
    ct_re = jnp.transpose(c_re, (0, 2, 1))[:, :, None, :]
    ct_im = jnp.transpose(c_im, (0, 2, 1))[:, :, None, :]
    w2_re = (ct_re * fr - ct_im * fi).reshape(nb, gb * P, T * H)
    w2_im = (ct_re * fi + ct_im * fr).reshape(nb, gb * P, T * H)

    rr = jnp.arange(T * H)
    cc = jnp.arange(T * LANES)
    same_i = (rr % H)[None, :, None] == (cc % H)[None, None, :]
    shifted = ((rr // H)[None, :, None] + jnp.arange(T)[:, None, None]
               == (cc // LANES)[None, None, :])
    rs = (same_i & shifted).astype(BF16)
    r3 = (jnp.arange(P)[:, None] == (jnp.arange(gb * P) % P)[None, :]).astype(BF16)

    def lanes(a):
        lead = a.shape[:-2]
        a = a.reshape(lead + (nb, gb * P))
        return jnp.moveaxis(a, -2, 0)

    a_re = lanes(aT_re)[:, None, :]
    a_im = lanes(aT_im)[:, None, :]
    p_re = lanes(pT_re)
    p_im = lanes(pT_im)
    a2_re = lanes(a2_re)[:, None, :]
    a2_im = lanes(a2_im)[:, None, :]
    return (base, w2_re, w2_im, rows3(v_re), rows3(v_im), rs, r3,
            a_re, a_im, p_re, p_im, a2_re, a2_im)


def _mixout_kernel(*refs, glu):
    if glu:
        (a_ref, y_ref, x_ref, wglu_ref, bglu_ref, wo_ref, g_ref, lng_ref, lnb_ref,
         o_ref) = refs
        y = y_ref[...].astype(F32)
        s = y * (0.5 * (1.0 + jnp.tanh(math.sqrt(2.0 / math.pi)
                                       * (y + 0.044715 * (y * y * y)))))
        z = _dot(s.astype(BF16), wglu_ref[...]) + bglu_ref[...]
        s2 = (s * _sigmoid(z)).astype(BF16)
        wa = a_ref.shape[1]
        mix = _dot(a_ref[...], wo_ref[0:wa, :]) + _dot(s2, wo_ref[wa:, :])
    else:
        a_ref, x_ref, wo_ref, g_ref, lng_ref, lnb_ref, o_ref = refs
        mix = _dot(a_ref[...], wo_ref[...])
    r = DEEPNORM_ALPHA * x_ref[...] + g_ref[...] * mix
    o_ref[...] = _layer_norm(r, lng_ref[...], lnb_ref[...])


def _mixout(a, y, x, wglu, bglu, wo, g, lng, lnb):
    L, d = x.shape
    tm = 512
    glu = y is not None
    row = lambda i: (i, 0)
    fix = lambda i: (0, 0)
    vec = pl.BlockSpec((1, d), fix)
    if glu:
        ws = y.shape[1]
        args = (a, y, x, wglu, bglu, wo, g, lng, lnb)
        in_specs = [pl.BlockSpec((tm, a.shape[1]), row), pl.BlockSpec((tm, ws), row),
                    pl.BlockSpec((tm, d), row), pl.BlockSpec((ws, ws), fix),
                    pl.BlockSpec((1, ws), fix), pl.BlockSpec(wo.shape, fix), vec, vec, vec]
    else:
        args = (a, x, wo, g, lng, lnb)
        in_specs = [pl.BlockSpec((tm, a.shape[1]), row), pl.BlockSpec((tm, d), row),
                    pl.BlockSpec(wo.shape, fix), vec, vec, vec]
    return pl.pallas_call(
        functools.partial(_mixout_kernel, glu=glu),
        grid=(L // tm,),
        in_specs=in_specs,
        out_specs=pl.BlockSpec((tm, d), row),
        out_shape=jax.ShapeDtypeStruct((L, d), F32),
        compiler_params=_cparams(1, 48),
        name="mixout_glu" if glu else "mixout",
    )(*args)


def _ffn_kernel(x_ref, sc_ref, sh_ref, wg_ref, wu_ref, wd_ref, g_ref, lng_ref, lnb_ref,
                o_ref, h_sc):
    j = pl.program_id(1)

    @pl.when(j == 0)
    def _():
        h_sc[...] = (x_ref[...] * (1.0 + sc_ref[...]) + sh_ref[...]).astype(BF16)
        o_ref[...] = jnp.zeros_like(o_ref)

    h = h_sc[...]
    gt = _dot(h, wg_ref[0])
    ut = _dot(h, wu_ref[0])
    act = (gt * _sigmoid(gt) * ut).astype(BF16)
    o_ref[...] += _dot(act, wd_ref[0])

    @pl.when(j == pl.num_programs(1) - 1)
    def _():
        r = DEEPNORM_ALPHA * x_ref[...] + g_ref[...] * o_ref[...]
        o_ref[...] = _layer_norm(r, lng_ref[...], lnb_ref[...])


def _ffn(x, sc, sh, wg, wu, wd, layer, g, lng, lnb):
    L, d = x.shape
    dff = wg.shape[2]
    tm, tf = 1024, 256
    row = lambda i, j: (i, 0)
    vec = pl.BlockSpec((1, d), lambda i, j: (0, 0))
    return pl.pallas_call(
        _ffn_kernel,
        grid=(L // tm, dff // tf),
        in_specs=[pl.BlockSpec((tm, d), row), vec, vec,
                  pl.BlockSpec((1, d, tf), lambda i, j: (layer, 0, j)),
                  pl.BlockSpec((1, d, tf), lambda i, j: (layer, 0, j)),
                  pl.BlockSpec((1, tf, d), lambda i, j: (layer, j, 0)),
                  vec, vec, vec],
        out_specs=pl.BlockSpec((tm, d), row),
        out_shape=jax.ShapeDtypeStruct((L, d), F32),
        scratch_shapes=[pltpu.VMEM((tm, d), BF16)],
        compiler_params=_cparams(2, 58),
        name="ffn",
    )(x, sc, sh, wg, wu, wd, g, lng, lnb)


def _inproj1_kernel(x_ref, sc_ref, sh_ref, posc_ref, posr_ref, freq_ref, slo_ref, shi_ref,
                    fcol_ref, wqt_ref, wk_ref, wvt_ref, qt_ref, k_ref, vt_ref,
                    *, slab, scale):
    half = ROT_DIM // 2
    dh = SWA_HEAD_DIM
    hb = (x_ref[...] * (1.0 + sc_ref[...]) + sh_ref[...]).astype(BF16)

    ang = fcol_ref[...] * posr_ref[...]
    cs = jnp.cos(ang)
    sn = jnp.sin(ang)
    for c in range(qt_ref.shape[0] // slab):
        r = _dot_nt(wqt_ref[c * slab:(c + 1) * slab, :], hb)
        for h in range(slab // dh):
            b = h * dh
            x1 = r[b:b + half]
            x2 = r[b + half:b + ROT_DIM]
            rot = jnp.concatenate([x1 * cs - x2 * sn, x2 * cs + x1 * sn], axis=0)
            qt_ref[c * slab + b:c * slab + b + ROT_DIM, :] = (rot * scale).astype(BF16)
            qt_ref[c * slab + b + ROT_DIM:c * slab + b + dh, :] = (
                r[b + ROT_DIM:b + dh] * scale).astype(BF16)

    kk = _dot(hb, wk_ref[...])
    ang = posc_ref[...] * freq_ref[...]
    cs = jnp.cos(ang)
    sn = jnp.sin(ang)
    sin_lo = sn * slo_ref[...]
    sin_hi = sn * shi_ref[...]
    for c in range(kk.shape[1] // LANES):
        xc = kk[:, c * LANES:(c + 1) * LANES]
        k_ref[:, c * LANES:(c + 1) * LANES] = (
            xc * cs + pltpu.roll(xc, LANES - half, 1) * sin_lo
            + pltpu.roll(xc, half, 1) * sin_hi).astype(BF16)
    vt_ref[...] = _dot_nt(wvt_ref[...], hb).astype(BF16)


def _inproj1(x, sc, sh, pos, w_qt, w_k, w_vt):
    L, d = x.shape
    n_q = w_qt.shape[0]
    n_k = w_k.shape[1]
    tm = 512
    half = ROT_DIM // 2
    inv_freq = jnp.power(jnp.float32(ROPE_THETA), -jnp.arange(half, dtype=F32) * (2.0 / ROT_DIM))
    dlane = jnp.arange(LANES) % SWA_HEAD_DIM
    freq = jnp.where(dlane < ROT_DIM, inv_freq[dlane % half], 0.0).reshape(1, LANES)
    slo = jnp.where(dlane < half, -1.0, 0.0).astype(F32).reshape(1, LANES)
    shi = jnp.where((dlane >= half) & (dlane < ROT_DIM), 1.0, 0.0).astype(F32).reshape(1, LANES)
    posf = pos.astype(F32)
    kern = functools.partial(_inproj1_kernel, slab=512, scale=LOG2E / math.sqrt(SWA_HEAD_DIM))

    def const(shape):
        return pl.BlockSpec(shape, lambda i: (0, 0), pipeline_mode=pl.Buffered(1))

    return pl.pallas_call(
        kern,
        grid=(L // tm,),
        in_specs=[pl.BlockSpec((tm, d), lambda i: (i, 0)),
                  const((1, d)), const((1, d)),
                  pl.BlockSpec((tm, 1), lambda i: (i, 0)),
                  pl.BlockSpec((1, tm), lambda i: (0, i)),
                  const((1, LANES)), const((1, LANES)), const((1, LANES)), const((half, 1)),
                  const((n_q, d)), const((d, n_k)), const((n_k, d))],
        out_specs=[pl.BlockSpec((n_q, tm), lambda i: (0, i)),
                   pl.BlockSpec((tm, n_k), lambda i: (i, 0)),
                   pl.BlockSpec((n_k, tm), lambda i: (0, i))],
        out_shape=[jax.ShapeDtypeStruct((n_q, L), BF16),
                   jax.ShapeDtypeStruct((L, n_k), BF16),
                   jax.ShapeDtypeStruct((n_k, L), BF16)],
        compiler_params=_cparams(1, 40),
        name="inproj1",
    )(x, sc, sh, posf.reshape(L, 1), posf.reshape(1, L), freq, slo, shi,
      inv_freq.reshape(half, 1), w_qt, w_k, w_vt)


SWA_BLOCKS_PER_STEP = 4


def _swa_kernel(sink_ref, qt_ref, kc_ref, kp_ref, vtc_ref, vtp_ref, o_ref, *, n_kv):
    n = pl.program_id(0)
    W = WINDOW
    dh = SWA_HEAD_DIM
    G = SWA_Q_PER_KV
    kj = lax.broadcasted_iota(jnp.int32, (2 * W, W), 0)
    qi = lax.broadcasted_iota(jnp.int32, (2 * W, W), 1)
    for qb in range(SWA_BLOCKS_PER_STEP):
        cur = slice(qb * W, (qb + 1) * W)
        prev = slice((qb - 1) * W, qb * W)
        lo_key = jnp.where(n > 0, qi, W - 1) if qb == 0 else qi
        band = jnp.logical_and(kj > lo_key, kj <= qi + W)
        mask1 = jnp.where(band, 0.0, NEG_BIG)
        mask = jnp.concatenate([mask1] * G, axis=1)
        for kvh in range(n_kv):
            ksl = slice(kvh * dh, (kvh + 1) * dh)
            k_prev = kp_ref[:, ksl] if qb == 0 else kc_ref[prev, ksl]
            vt_prev = vtp_ref[ksl, :] if qb == 0 else vtc_ref[ksl, prev]
            kk = jnp.concatenate([k_prev, kc_ref[cur, ksl]], axis=0)
            vv = jnp.concatenate([vt_prev, vtc_ref[ksl, cur]], axis=1)
            q = jnp.concatenate([qt_ref[(kvh * G + g) * dh:(kvh * G + g + 1) * dh, cur]
                                 for g in range(G)], axis=1)
            s = _dot(kk, q) + mask
            sink = jnp.concatenate([jnp.full((1, W), sink_ref[kvh * G + g] * LOG2E, F32)
                                    for g in range(G)], axis=1)
            m = jnp.maximum(jnp.max(s, axis=0, keepdims=True), sink)
            p = jnp.exp2(s - m)
            den = jnp.sum(p, axis=0, keepdims=True) + jnp.exp2(sink - m)
            o = _dot(vv, p.astype(BF16)) / den
            for gp in range(G // 2):
                two = jnp.concatenate([o[:, (2 * gp) * W:(2 * gp + 1) * W],
                                       o[:, (2 * gp + 1) * W:(2 * gp + 2) * W]], axis=0)
                lo = (kvh * G + 2 * gp) * dh
                o_ref[cur, lo:lo + 2 * dh] = two.T.astype(BF16)


def _swa(qt, k, vt, sinks):
    n_q, L = qt.shape
    kw = k.shape[1]
    n_kv = kw // SWA_HEAD_DIM
    W = WINDOW
    nb = SWA_BLOCKS_PER_STEP
    S = nb * W
    prev_blk = lambda n: jnp.maximum(nb * n - 1, 0)
    grid_spec = pltpu.PrefetchScalarGridSpec(
        num_scalar_prefetch=1,
        grid=(L // S,),
        in_specs=[pl.BlockSpec((n_q, S), lambda n, s: (0, n)),
                  pl.BlockSpec((S, kw), lambda n, s: (n, 0)),
                  pl.BlockSpec((W, kw), lambda n, s: (prev_blk(n), 0)),
                  pl.BlockSpec((kw, S), lambda n, s: (0, n)),
                  pl.BlockSpec((kw, W), lambda n, s: (0, prev_blk(n)))],
        out_specs=pl.BlockSpec((S, n_q), lambda n, s: (n, 0)),
    )
    return pl.pallas_call(
        functools.partial(_swa_kernel, n_kv=n_kv),
        grid_spec=grid_spec,
        out_shape=jax.ShapeDtypeStruct((L, n_q), BF16),
        compiler_params=_cparams(1, 32),
        name="swa",
    )(sinks, qt, k, k, vt, vt)


def kernel(x, c, positions, w_in_ab, b_forget, ssm_lambda_re, ssm_lambda_im, ssm_log_dt, ssm_b_re, ssm_b_im, ssm_c_re, ssm_c_im, ssm_d, w_glu, b_glu, w_out_ab, w_in_c, attn_sinks, w_out_c, w_ada, b_ada, ln_mix_g, ln_mix_b, ln_ffn_g, ln_ffn_b, w_ffn_gate, w_ffn_up, w_ffn_down):
    B, L, D = x.shape
    assert B == 1
    x2 = x.reshape(L, D)
    mod = _ada(c, w_ada, b_ada)

    def mods(layer):
        return [mod[layer, :, k * D:(k + 1) * D] for k in range(6)]

    def vec(a):
        return a.reshape(1, -1)

    sh1, sc1, g1, sh2, sc2, g2 = mods(0)
    W = FOX_WIDTH
    w_in = w_in_ab[0]
    w_qku = jnp.concatenate([w_in[:, :2 * W], w_in[:, 3 * W + FOX_HEADS:]], 1).astype(BF16)
    w_vt = w_in[:, 2 * W:3 * W].T.astype(BF16)
    w_f = jnp.pad(w_in[:, 3 * W:3 * W + FOX_HEADS], ((0, 0), (0, LANES - FOX_HEADS))).astype(BF16)
    b_f = jnp.pad(b_forget[0], (0, LANES - FOX_HEADS)).reshape(1, LANES)
    tq = 512
    qa, ka, u, vt = _inproj0(x2, sc1, sh1, w_qku, w_vt, w_f, b_f, tq)
    a_out = _fox(qa, ka, vt, tq)

    T = SSM_CHUNK
    nrow = L // T
    nsub = nrow // SUBLANES
    width = u.shape[1]
    u_s = u.reshape(SUBLANES, nsub, T, width).transpose(2, 1, 0, 3).reshape(T, nrow, width)
    tables = _ssm_tables(ssm_lambda_re[0], ssm_lambda_im[0], ssm_log_dt[0], ssm_b_re[0],
                         ssm_b_im[0], ssm_c_re[0], ssm_c_im[0], ssm_d[0], nsub)
    y_s = _ssm(u_s, *tables)
    y = y_s.reshape(T, nsub, SUBLANES, width).transpose(2, 1, 0, 3).reshape(L, width)

    x2 = _mixout(a_out, y, x2, w_glu[0].astype(BF16), vec(b_glu[0]), w_out_ab[0].astype(BF16),
                 g1, vec(ln_mix_g[0]), vec(ln_mix_b[0]))
    wg, wu, wd = (w.astype(BF16) for w in (w_ffn_gate, w_ffn_up, w_ffn_down))
    x2 = _ffn(x2, sc2, sh2, wg, wu, wd, 0, g2, vec(ln_ffn_g[0]), vec(ln_ffn_b[0]))

    sh1, sc1, g1, sh2, sc2, g2 = mods(1)
    n_q = w_out_c.shape[1]
    n_k = (w_in_c.shape[2] - n_q) // 2
    w_c = w_in_c[0]
    a1 = _swa(*_inproj1(x2, sc1, sh1, positions, w_c[:, :n_q].T.astype(BF16),
                        w_c[:, n_q:n_q + n_k].astype(BF16), w_c[:, n_q + n_k:].T.astype(BF16)),
              attn_sinks[0])
    x2 = _mixout(a1, None, x2, None, None, w_out_c[0].astype(BF16), g1,
                 vec(ln_mix_g[1]), vec(ln_mix_b[1]))
    x2 = _ffn(x2, sc2, sh2, wg, wu, wd, 1, g2, vec(ln_ffn_g[1]), vec(ln_ffn_b[1]))
    return x2.reshape(B, L, D)
```
